```python
import math
import jax
import jax.numpy as jnp
from jax import lax
import numpy as np

D_MODEL = 2048
BATCH = 2
SEQ = 8192
DEPTH = 4

D_FF = 5632
FFN_RES = 0.5
NORM_EPS = 1e-6
Q_BLOCK = 128
NEG_INF = -1e30
REL_BUCKETS = 32
REL_MAX_DIST = 128
ATT_HEADS = D_MODEL // 256
NSA_HEADS = ATT_HEADS
NSA_HEAD_DIM = 128
NSA_KV_GROUPS = 2
NSA_CMP_LEN = 32
NSA_CMP_STRIDE = 16
NSA_CMP_HIDDEN = 256
NSA_SEL_LEN = 64
NSA_SEL_BLOCKS = 16
NSA_WINDOW = 512
NSA_FORCE = 1e4
SSD_HEAD_DIM = 64
SSD_HEADS = D_MODEL // SSD_HEAD_DIM
SSD_GROUPS = 4
SSD_STATE = 128
SSD_CONV = 4
SSD_CHUNK = 256
SSD_INNER = SSD_HEADS * SSD_HEAD_DIM
SSD_CONV_DIM = SSD_INNER + 2 * SSD_GROUPS * SSD_STATE
DSA_HEADS = ATT_HEADS
DSA_HEAD_DIM = 128
DSA_Q_RANK = 512
DSA_KV_RANK = 256
DSA_IDX_HEADS = 16
DSA_IDX_DIM = 64
DSA_TOPK = 256
RWKV_HEAD_DIM = 64
RWKV_HEADS = D_MODEL // RWKV_HEAD_DIM
RWKV_DIM = RWKV_HEADS * RWKV_HEAD_DIM
RWKV_W_LORA = 64
RWKV_A_LORA = 64
RWKV_G_LORA = 256
RWKV_LN_EPS = 64e-5
RWKV_PROJ = 3 * RWKV_DIM + RWKV_W_LORA + RWKV_A_LORA + RWKV_G_LORA
NSA_Q = NSA_HEADS * NSA_HEAD_DIM
NSA_KV = NSA_KV_GROUPS * NSA_HEAD_DIM
EVEN_SPLITS = (NSA_Q,) + (NSA_KV,) * 6 + (3 * NSA_HEADS, SSD_INNER, SSD_CONV_DIM, SSD_HEADS)
EVEN_IN = sum(EVEN_SPLITS)
EVEN_MIX = NSA_Q + SSD_INNER
ODD_SPLITS = (DSA_Q_RANK, DSA_KV_RANK, DSA_IDX_DIM, DSA_IDX_HEADS, RWKV_PROJ)
ODD_IN = sum(ODD_SPLITS)
ODD_MIX = DSA_HEADS * DSA_HEAD_DIM + RWKV_DIM

kernel_name = 'hybrid_nsa_ssd_dsa_rwkv7_macaron'


def _split(p, sizes):
    return jnp.split(p, np.cumsum(sizes)[:-1].tolist(), axis=-1)


def rms_norm(x, g, eps=NORM_EPS):
    xf = x.astype(jnp.float32)
    y = xf * lax.rsqrt(jnp.mean(xf * xf, axis=-1, keepdims=True) + eps)
    return (y * g.astype(jnp.float32)).astype(x.dtype)


def swiglu(u, wg, wu, wd):
    return (jax.nn.silu(u @ wg) * (u @ wu)) @ wd


def masked_softmax(logits, mask):
    logits = jnp.where(mask, logits.astype(jnp.float32), NEG_INF)
    return jnp.where(mask, jax.nn.softmax(logits, axis=-1), 0.0)


def rel_bucket(dist):
    n = jnp.maximum(dist, 0)
    exact = REL_BUCKETS // 2
    nf = jnp.maximum(n, 1).astype(jnp.float32)
    large = exact + (jnp.log(nf / exact) / math.log(REL_MAX_DIST / exact) * (REL_BUCKETS - exact)).astype(jnp.int32)
    large = jnp.minimum(large, REL_BUCKETS - 1)
    return jnp.where(n < exact, n, large)


def causal_depthwise_conv(u, w, b):
    k = w.shape[0]
    out = lax.conv_general_dilated(u, w[:, None, :].astype(u.dtype), window_strides=(1,), padding=[(k - 1, 0)],
                                   dimension_numbers=('NWC', 'WIO', 'NWC'), feature_group_count=u.shape[-1])
    return out + b


def nsa_attention(q, kc_tok, vc_tok, ks_tok, vs_tok, kw_tok, vw_tok, gates, rel_bias, pe_k, pe_v, w1_k, w2_k, w1_v, w2_v):
    B, T, H, Dh = q.shape
    G = ks_tok.shape[2]
    R = H // G
    n_cmp = (T - NSA_CMP_LEN) // NSA_CMP_STRIDE + 1
    cmp_tok = NSA_CMP_STRIDE * np.arange(n_cmp)[:, None] + np.arange(NSA_CMP_LEN)[None, :]
    cmp_end = jnp.asarray(cmp_tok[:, -1], jnp.int32)

    def compress(u, pe, w1, w2):
        blk = u[:, cmp_tok] + pe[:, None, :]
        blk = jnp.moveaxis(blk, 3, 2).reshape(B, n_cmp, G, NSA_CMP_LEN * Dh)
        return jax.nn.silu(blk @ w1) @ w2

    kc = compress(kc_tok, pe_k, w1_k, w2_k)
    vc = compress(vc_tok, pe_v, w1_v, w2_v)
    n_sel = T // NSA_SEL_LEN
    n_top = min(NSA_SEL_BLOCKS, n_sel)
    sel_start = NSA_SEL_LEN * np.arange(n_sel)
    overlap = jnp.asarray(((cmp_tok[:, :1] < sel_start[None, :] + NSA_SEL_LEN) &
                           (cmp_tok[:, -1:] >= sel_start[None, :])).astype(np.float32))
    ks_blk = ks_tok.reshape(B, n_sel, NSA_SEL_LEN, G, Dh).transpose(0, 3, 1, 2, 4)
    vs_blk = vs_tok.reshape(B, n_sel, NSA_SEL_LEN, G, Dh).transpose(0, 3, 1, 2, 4)
    kw_pad = jnp.pad(kw_tok, ((0, 0), (NSA_WINDOW, 0), (0, 0), (0, 0)))
    vw_pad = jnp.pad(vw_tok, ((0, 0), (NSA_WINDOW, 0), (0, 0), (0, 0)))
    qg = (q * Dh ** -0.5).reshape(B, T, G, R, Dh)
    gg = gates.reshape(B, T, G, R, 3)
    tbl = rel_bias.reshape(REL_BUCKETS, G, R)
    tbl_g = jnp.transpose(tbl, (1, 0, 2))
    gather_blocks = jax.vmap(jax.vmap(lambda kb, ib: kb[ib]))

    def head_bias(dist):
        return jnp.moveaxis(tbl[rel_bucket(dist)], (-2, -1), (0, 1))

    def block(i):
        t0 = i * Q_BLOCK
        t = t0 + jnp.arange(Q_BLOCK)
        qi = lax.dynamic_slice_in_dim(qg, t0, Q_BLOCK, axis=1)
        gi = lax.dynamic_slice_in_dim(gg, t0, Q_BLOCK, axis=1)
        mask_c = cmp_end[None, :] <= t[:, None]
        s_c = jnp.einsum('bqgrd,bngd->bgrqn', qi, kc) + head_bias(t[:, None] - cmp_end[None, :])
        p_c = masked_softmax(s_c, mask_c)
        o_c = jnp.einsum('bgrqn,bngd->bqgrd', p_c.astype(vc.dtype), vc)
        imp = jnp.einsum('bgrqn,nj->bgqj', p_c, overlap)
        blk_id = jnp.arange(n_sel)
        cur = t // NSA_SEL_LEN
        forced = (blk_id[None, :] == 0) | (blk_id[None, :] == cur[:, None]) | (blk_id[None, :] == cur[:, None] - 1)
        adm = blk_id[None, :] * NSA_SEL_LEN <= t[:, None]
        score = jnp.where(adm, imp + NSA_FORCE * forced.astype(jnp.float32), NEG_INF)
        _, top = lax.top_k(score, n_top)
        ks = gather_blocks(ks_blk, top).reshape(B, G, Q_BLOCK, n_top * NSA_SEL_LEN, Dh)
        vs = gather_blocks(vs_blk, top).reshape(B, G, Q_BLOCK, n_top * NSA_SEL_LEN, Dh)
        pos_s = (top[..., None] * NSA_SEL_LEN + jnp.arange(NSA_SEL_LEN)).reshape(B, G, Q_BLOCK, -1)
        dist_s = t[None, None, :, None] - pos_s
        bias_s = jax.vmap(lambda tb, bk: tb[bk], in_axes=(0, 1), out_axes=1)(tbl_g, rel_bucket(dist_s))
        s_s = jnp.einsum('bqgrd,bgqkd->bgrqk', qi, ks) + jnp.moveaxis(bias_s, -1, 2)
        p_s = masked_softmax(s_s, (dist_s >= 0)[:, :, None])
        o_s = jnp.einsum('bgrqk,bgqkd->bqgrd', p_s.astype(vs.dtype), vs)
        kwi = lax.dynamic_slice_in_dim(kw_pad, t0, Q_BLOCK + NSA_WINDOW, axis=1)
        vwi = lax.dynamic_slice_in_dim(vw_pad, t0, Q_BLOCK + NSA_WINDOW, axis=1)
        pos_w = t0 - NSA_WINDOW + jnp.arange(Q_BLOCK + NSA_WINDOW)
        dist_w = t[:, None] - pos_w[None, :]
        mask_w = (dist_w >= 0) & (dist_w < NSA_WINDOW) & (pos_w[None, :] >= 0)
        s_w = jnp.einsum('bqgrd,bkgd->bgrqk', qi, kwi) + head_bias(dist_w)
        p_w = masked_softmax(s_w, mask_w)
        o_w = jnp.einsum('bgrqk,bkgd->bqgrd', p_w.astype(vwi.dtype), vwi)
        o = gi[..., 0:1] * o_c + gi[..., 1:2] * o_s + gi[..., 2:3] * o_w
        return o.reshape(B, Q_BLOCK, H * Dh)

    out = lax.map(block, jnp.arange(T // Q_BLOCK))
    return jnp.moveaxis(out, 0, 1).reshape(B, T, H * Dh)


def ssd_mixer(z, xbc, dt_raw, conv_w, conv_b, dt_bias, a_log, d_skip, norm_g):
    B, T, _ = z.shape
    H, P, G, N = SSD_HEADS, SSD_HEAD_DIM, SSD_GROUPS, SSD_STATE
    R = H // G
    f32 = jnp.float32
    xbc = jax.nn.silu(causal_depthwise_conv(xbc, conv_w, conv_b))
    xs, bm, cm = jnp.split(xbc, [H * P, H * P + G * N], axis=-1)
    xs = xs.reshape(B, T, G, R, P).astype(f32)
    bm = bm.reshape(B, T, G, N).astype(f32)
    cm = cm.reshape(B, T, G, N).astype(f32)
    dt = jax.nn.softplus(dt_raw.astype(f32) + dt_bias.astype(f32)).reshape(B, T, G, R)
    a = -jnp.exp(a_log.astype(f32)).reshape(G, R)
    lc = math.gcd(T, SSD_CHUNK)
    nc = T // lc
    causal = jnp.tril(jnp.ones((lc, lc), dtype=bool))

    def chunks(u):
        return jnp.moveaxis(u.reshape((B, nc, lc) + u.shape[2:]), 1, 0)

    def step(state, inp):
        xc, dtc, bc, cc = inp
        acum = jnp.cumsum(dtc * a, axis=1)
        seg = acum[:, :, None] - acum[:, None, :]
        decay = jnp.exp(jnp.where(causal[None, :, :, None, None], seg, -jnp.inf))
        xdt = xc * dtc[..., None]
        cb = jnp.einsum('btgn,bsgn->btsg', cc, bc)
        y = jnp.einsum('btsg,btsgr,bsgrp->btgrp', cb, decay, xdt)
        y = y + jnp.einsum('btgn,bgrpn->btgrp', cc, state) * jnp.exp(acum)[..., None]
        to_end = jnp.exp(acum[:, -1:] - acum)
        state = state * jnp.exp(acum[:, -1])[..., None, None] + jnp.einsum('bsgn,bsgr,bsgrp->bgrpn', bc, to_end, xdt)
        return state, y

    state0 = jnp.zeros((B, G, R, P, N), f32)
    _, y = lax.scan(step, state0, (chunks(xs), chunks(dt), chunks(bm), chunks(cm)))
    y = jnp.moveaxis(y, 0, 1).reshape(B, T, G, R, P)
    y = y + d_skip.astype(f32).reshape(G, R, 1) * xs
    y = y.reshape(B, T, H * P) * jax.nn.silu(z.astype(f32))
    yg = y.reshape(B, T, G, -1)
    yg = yg * lax.rsqrt(jnp.mean(yg * yg, axis=-1, keepdims=True) + NORM_EPS)
    y = yg.reshape(B, T, H * P) * norm_g.astype(f32)
    return y.astype(z.dtype)


def dsa_attention(c_q, c_kv, k_idx, w_idx, rel_bias, q_norm_g, kv_norm_g, w_uq, w_qidx, w_uk, w_uv):
    B, T, _ = c_q.shape
    H, Dh = DSA_HEADS, DSA_HEAD_DIM
    c_q = rms_norm(c_q, q_norm_g)
    c_kv = rms_norm(c_kv, kv_norm_g)
    q = jnp.einsum('btc,chd->bthd', c_q, w_uq)
    q_abs = jnp.einsum('bthd,chd->bthc', q, w_uk) * Dh ** -0.5
    q_idx = jnp.einsum('btc,chd->bthd', c_q, w_qidx)
    w_idx = w_idx * (DSA_IDX_HEADS ** -0.5 * DSA_IDX_DIM ** -0.5)
    n_keep = min(DSA_TOPK, T // 4)
    gather_keys = jax.vmap(lambda c, i: c[i])
    s_pos = jnp.arange(T)

    def block(i):
        t0 = i * Q_BLOCK
        t = t0 + jnp.arange(Q_BLOCK)
        qi = lax.dynamic_slice_in_dim(q_idx, t0, Q_BLOCK, axis=1)
        wi = lax.dynamic_slice_in_dim(w_idx, t0, Q_BLOCK, axis=1)
        qa = lax.dynamic_slice_in_dim(q_abs, t0, Q_BLOCK, axis=1)
        idx_score = jax.nn.relu(jnp.einsum('bqhd,bsd->bqhs', qi, k_idx))
        idx_score = jnp.einsum('bqhs,bqh->bqs', idx_score, wi)
        idx_score = jnp.where(s_pos[None, None, :] <= t[None, :, None], idx_score.astype(jnp.float32), NEG_INF)
        _, top = lax.top_k(idx_score, n_keep)
        ckv = gather_keys(c_kv, top)
        dist = t[None, :, None] - top
        bias = jnp.moveaxis(rel_bias[rel_bucket(dist)], -1, 1)
        logits = jnp.einsum('bqhc,bqkc->bhqk', qa, ckv) + bias
        p = masked_softmax(logits, (dist >= 0)[:, None])
        o = jnp.einsum('bhqk,bqkc->bqhc', p.astype(ckv.dtype), ckv)
        return jnp.einsum('bqhc,chd->bqhd', o, w_uv).reshape(B, Q_BLOCK, H * Dh)

    out = lax.map(block, jnp.arange(T // Q_BLOCK))
    return jnp.moveaxis(out, 0, 1).reshape(B, T, H * Dh)


def rwkv7_mixer(proj, mu, w0, w2, a0, a2, g2, k_k, k_a, r_k, ln_g, ln_b):
    B, T, _ = proj.shape
    H, N, C = RWKV_HEADS, RWKV_HEAD_DIM, RWKV_DIM
    f32 = jnp.float32
    p = proj.astype(f32)
    prev = jnp.pad(p, ((0, 0), (1, 0), (0, 0)))[:, :-1]
    p = p + (prev - p) * mu
    r, k, v, wl, al, gl = _split(p, (C, C, C, RWKV_W_LORA, RWKV_A_LORA, RWKV_G_LORA))
    w = -jax.nn.softplus(-(w0 + jnp.tanh(wl) @ w2)) - 0.5
    decay = jnp.exp(-jnp.exp(w))
    a = jax.nn.sigmoid(a0 + al @ a2)
    g = jax.nn.sigmoid(gl) @ g2
    kk = (k * k_k).reshape(B, T, H, N)
    kk = kk / jnp.maximum(jnp.sqrt(jnp.sum(kk * kk, axis=-1, keepdims=True)), 1e-12)
    k = k * (1.0 + (a - 1.0) * k_a)

    def heads(u):
        return jnp.moveaxis(u.reshape(B, T, H, N).astype(f32), 1, 0)

    def step(s, inp):
        rt, wt, kt, vt, kkt, at = inp
        s = (s * wt[:, :, None, :]
             + jnp.einsum('bhvk,bhk->bhv', s, -kkt)[..., None] * (kkt * at)[:, :, None, :]
             + vt[..., None] * kt[:, :, None, :])
        return s, jnp.einsum('bhvk,bhk->bhv', s, rt)

    s0 = jnp.zeros((B, H, N, N), f32)
    _, y = lax.scan(step, s0, (heads(r), heads(decay), heads(k), heads(v), jnp.moveaxis(kk, 1, 0), heads(a)))
    y = jnp.moveaxis(y, 0, 1)
    mean = jnp.mean(y, axis=-1, keepdims=True)
    var = jnp.mean(jnp.square(y - mean), axis=-1, keepdims=True)
    y = ((y - mean) * lax.rsqrt(var + RWKV_LN_EPS)).reshape(B, T, C) * ln_g + ln_b
    rh, kh, vh = r.reshape(B, T, H, N), k.reshape(B, T, H, N), v.reshape(B, T, H, N)
    y = y + (jnp.sum(rh * kh * r_k, axis=-1, keepdims=True) * vh).reshape(B, T, C)
    return (y * g).astype(proj.dtype)


def even_mixer(u, w_in, w_out, rel_bias, pe_k, pe_v, w1_k, w2_k, w1_v, w2_v,
               conv_w, conv_b, dt_bias, a_log, d_skip, ssd_norm_g):
    B, T, _ = u.shape
    p = u @ w_in
    q, kc, vc, ks, vs, kw, vw, gates, z, xbc, dt = _split(p, EVEN_SPLITS)

    def kv(t):
        return t.reshape(B, T, NSA_KV_GROUPS, NSA_HEAD_DIM)

    o_a = nsa_attention(q.reshape(B, T, NSA_HEADS, NSA_HEAD_DIM), kv(kc), kv(vc), kv(ks), kv(vs), kv(kw), kv(vw),
                        jax.nn.sigmoid(gates.reshape(B, T, NSA_HEADS, 3)), rel_bias, pe_k, pe_v, w1_k, w2_k, w1_v, w2_v)
    o_b = ssd_mixer(z, xbc, dt, conv_w, conv_b, dt_bias, a_log, d_skip, ssd_norm_g)
    return jnp.concatenate([o_a.astype(u.dtype), o_b], axis=-1) @ w_out


def odd_mixer(u, w_in, w_out, rel_bias, q_norm_g, kv_norm_g, w_uq, w_qidx, w_uk, w_uv,
              mu, w0, w2, a0, a2, g2, k_k, k_a, r_k, ln_g, ln_b):
    p = u @ w_in
    c_q, c_kv, k_idx, w_idx, rw = _split(p, ODD_SPLITS)
    o_c = dsa_attention(c_q, c_kv, k_idx, w_idx, rel_bias, q_norm_g, kv_norm_g, w_uq, w_qidx, w_uk, w_uv)
    o_d = rwkv7_mixer(rw, mu, w0, w2, a0, a2, g2, k_k, k_a, r_k, ln_g, ln_b)
    return jnp.concatenate([o_c.astype(u.dtype), o_d], axis=-1) @ w_out


def setup_inputs(seed: int = 0) -> dict:
    key = jax.random.key(seed)
    keys = iter(jax.random.split(key, 64))
    f32 = jnp.float32

    def nrm(shape, scale):
        return scale * jax.random.normal(next(keys), shape, f32)

    def gain(shape):
        return 1.0 + 0.1 * jax.random.normal(next(keys), shape, f32)

    def unif(shape, lo, hi):
        return jax.random.uniform(next(keys), shape, f32, lo, hi)

    ne = (DEPTH + 1) // 2
    no = DEPTH // 2
    dt0 = jnp.exp(unif((ne, SSD_HEADS), math.log(1e-3), math.log(1e-1)))
    return {
        'x': nrm((BATCH, SEQ, D_MODEL), 1.0),
        'norm_g': gain((DEPTH, 6, D_MODEL)),
        'ffn_w_gate': nrm((DEPTH, 2, D_MODEL, D_FF), D_MODEL ** -0.5),
        'ffn_w_up': nrm((DEPTH, 2, D_MODEL, D_FF), D_MODEL ** -0.5),
        'ffn_w_down': nrm((DEPTH, 2, D_FF, D_MODEL), D_FF ** -0.5),
        'rel_bias': nrm((REL_BUCKETS, ATT_HEADS), 0.5),
        'even_w_in': nrm((ne, D_MODEL, EVEN_IN), D_MODEL ** -0.5),
        'even_w_out': nrm((ne, EVEN_MIX, D_MODEL), EVEN_MIX ** -0.5),
        'nsa_pe_k': nrm((ne, NSA_CMP_LEN, NSA_HEAD_DIM), 0.5),
        'nsa_pe_v': nrm((ne, NSA_CMP_LEN, NSA_HEAD_DIM), 0.5),
        'nsa_cmp_w1_k': nrm((ne, NSA_CMP_LEN * NSA_HEAD_DIM, NSA_CMP_HIDDEN), (NSA_CMP_LEN * NSA_HEAD_DIM) ** -0.5),
        'nsa_cmp_w2_k': nrm((ne, NSA_CMP_HIDDEN, NSA_HEAD_DIM), NSA_CMP_HIDDEN ** -0.5),
        'nsa_cmp_w1_v': nrm((ne, NSA_CMP_LEN * NSA_HEAD_DIM, NSA_CMP_HIDDEN), (NSA_CMP_LEN * NSA_HEAD_DIM) ** -0.5),
        'nsa_cmp_w2_v': nrm((ne, NSA_CMP_HIDDEN, NSA_HEAD_DIM), NSA_CMP_HIDDEN ** -0.5),
        'ssd_conv_w': nrm((ne, SSD_CONV, SSD_CONV_DIM), SSD_CONV ** -0.5),
        'ssd_conv_b': nrm((ne, SSD_CONV_DIM), 0.1),
        'ssd_dt_bias': dt0 + jnp.log(-jnp.expm1(-dt0)),
        'ssd_a_log': jnp.log(unif((ne, SSD_HEADS), 1.0, 16.0)),
        'ssd_d': gain((ne, SSD_HEADS)),
        'ssd_norm_g': gain((ne, SSD_INNER)),
        'odd_w_in': nrm((no, D_MODEL, ODD_IN), D_MODEL ** -0.5),
        'odd_w_out': nrm((no, ODD_MIX, D_MODEL), ODD_MIX ** -0.5),
        'dsa_q_norm_g': gain((no, DSA_Q_RANK)),
        'dsa_kv_norm_g': gain((no, DSA_KV_RANK)),
        'dsa_w_uq': nrm((no, DSA_Q_RANK, DSA_HEADS, DSA_HEAD_DIM), DSA_Q_RANK ** -0.5),
        'dsa_w_qidx': nrm((no, DSA_Q_RANK, DSA_IDX_HEADS, DSA_IDX_DIM), DSA_Q_RANK ** -0.5),
        'dsa_w_uk': nrm((no, DSA_KV_RANK, DSA_HEADS, DSA_HEAD_DIM), DSA_KV_RANK ** -0.5),
        'dsa_w_uv': nrm((no, DSA_KV_RANK, DSA_HEADS, DSA_HEAD_DIM), DSA_KV_RANK ** -0.5),
        'rwkv_mu': unif((no, RWKV_PROJ), 0.0, 1.0),
        'rwkv_w0': unif((no, RWKV_DIM), -3.0, 1.0),
        'rwkv_w2': nrm((no, RWKV_W_LORA, RWKV_DIM), 0.5 * RWKV_W_LORA ** -0.5),
        'rwkv_a0': nrm((no, RWKV_DIM), 0.5),
        'rwkv_a2': nrm((no, RWKV_A_LORA, RWKV_DIM), RWKV_A_LORA ** -0.5),
        'rwkv_g2': nrm((no, RWKV_G_LORA, RWKV_DIM), RWKV_G_LORA ** -0.5),
        'rwkv_k_k': gain((no, RWKV_DIM)),
        'rwkv_k_a': gain((no, RWKV_DIM)),
        'rwkv_r_k': nrm((no, RWKV_HEADS, RWKV_HEAD_DIM), 0.1),
        'rwkv_ln_g': gain((no, RWKV_DIM)),
        'rwkv_ln_b': nrm((no, RWKV_DIM), 0.01),
    }


def reference(x, norm_g, ffn_w_gate, ffn_w_up, ffn_w_down, rel_bias, even_w_in, even_w_out,
              nsa_pe_k, nsa_pe_v, nsa_cmp_w1_k, nsa_cmp_w2_k, nsa_cmp_w1_v, nsa_cmp_w2_v,
              ssd_conv_w, ssd_conv_b, ssd_dt_bias, ssd_a_log, ssd_d, ssd_norm_g,
              odd_w_in, odd_w_out, dsa_q_norm_g, dsa_kv_norm_g, dsa_w_uq, dsa_w_qidx, dsa_w_uk, dsa_w_uv,
              rwkv_mu, rwkv_w0, rwkv_w2, rwkv_a0, rwkv_a2, rwkv_g2, rwkv_k_k, rwkv_k_a, rwkv_r_k,
              rwkv_ln_g, rwkv_ln_b):
    h = x
    for layer in range(DEPTH):
        g = norm_g[layer]
        i = layer // 2
        f = swiglu(rms_norm(h, g[0]), ffn_w_gate[layer, 0], ffn_w_up[layer, 0], ffn_w_down[layer, 0])
        h = h + FFN_RES * rms_norm(f, g[1])
        u = rms_norm(h, g[2])
        if layer % 2 == 0:
            m = even_mixer(u, even_w_in[i], even_w_out[i], rel_bias, nsa_pe_k[i], nsa_pe_v[i],
                           nsa_cmp_w1_k[i], nsa_cmp_w2_k[i], nsa_cmp_w1_v[i], nsa_cmp_w2_v[i],
                           ssd_conv_w[i], ssd_conv_b[i], ssd_dt_bias[i], ssd_a_log[i], ssd_d[i], ssd_norm_g[i])
        else:
            m = odd_mixer(u, odd_w_in[i], odd_w_out[i], rel_bias, dsa_q_norm_g[i], dsa_kv_norm_g[i],
                          dsa_w_uq[i], dsa_w_qidx[i], dsa_w_uk[i], dsa_w_uv[i],
                          rwkv_mu[i], rwkv_w0[i], rwkv_w2[i], rwkv_a0[i], rwkv_a2[i], rwkv_g2[i],
                          rwkv_k_k[i], rwkv_k_a[i], rwkv_r_k[i], rwkv_ln_g[i], rwkv_ln_b[i])
        h = h + rms_norm(m, g[3])
        f = swiglu(rms_norm(h, g[4]), ffn_w_gate[layer, 1], ffn_w_up[layer, 1], ffn_w_down[layer, 1])
        h = h + FFN_RES * rms_norm(f, g[5])
    return h
```

```python
import functools
import math

import jax
import jax.numpy as jnp
import numpy as np
from jax import lax
from jax.experimental import pallas as pl
from jax.experimental.pallas import tpu as pltpu

F32 = jnp.float32
BF16 = jnp.bfloat16

NORM_EPS = 1e-6
FFN_RES = 0.5
NEG_INF = -1e30
REL_BUCKETS = 32
REL_MAX_DIST = 128
Q_BLOCK = 128

NSA_HEADS = 8
NSA_HEAD_DIM = 128
NSA_KV_GROUPS = 2
NSA_CMP_LEN = 32
NSA_CMP_STRIDE = 16
NSA_SEL_LEN = 64
NSA_SEL_BLOCKS = 16
NSA_WINDOW = 512
NSA_FORCE = 1e4

SSD_HEAD_DIM = 64
SSD_HEADS = 32
SSD_GROUPS = 4
SSD_STATE = 128
SSD_CONV = 4
SSD_CHUNK = 256

DSA_HEADS = 8
DSA_HEAD_DIM = 128
DSA_Q_RANK = 512
DSA_KV_RANK = 256
DSA_IDX_HEADS = 16
DSA_IDX_DIM = 64
DSA_TOPK = 256

RWKV_HEAD_DIM = 64
RWKV_HEADS = 32
RWKV_W_LORA = 64
RWKV_A_LORA = 64
RWKV_G_LORA = 256
RWKV_LN_EPS = 64e-5
RWKV_CHUNK = 64

VMEM_LIMIT_V7X = 56 * 1024 * 1024
LANES = 128


def _cparams(sem):
    return pltpu.CompilerParams(dimension_semantics=sem, vmem_limit_bytes=VMEM_LIMIT_V7X)


def _rms(x, g, eps=NORM_EPS):
    return x * lax.rsqrt(jnp.mean(x * x, axis=-1, keepdims=True) + eps) * g


def _dot(a, b):
    return jnp.dot(a, b, preferred_element_type=F32)


def _dot_t(a, b):
    return lax.dot_general(a, b, (((1,), (1,)), ((), ())), preferred_element_type=F32)


def _dot_hi(a, b):
    return jnp.dot(a, b, preferred_element_type=F32, precision=lax.Precision.HIGHEST)


def _ffn_body(x_ref, gin_ref, wg_ref, wu_ref, wd_ref, gout_ref, o_ref, xn_ref, acc_ref):
    j = pl.program_id(1)

    @pl.when(j == 0)
    def _():
        xn_ref[...] = _rms(x_ref[...], gin_ref[...]).astype(BF16)
        acc_ref[...] = jnp.zeros_like(acc_ref)

    xn = xn_ref[...]
    g = _dot(xn, wg_ref[...])
    u = _dot(xn, wu_ref[...])
    h = (g * jax.nn.sigmoid(g)) * u
    acc_ref[...] += _dot(h.astype(BF16), wd_ref[...])

    @pl.when(j == pl.num_programs(1) - 1)
    def _():
        o_ref[...] = x_ref[...] + FFN_RES * _rms(acc_ref[...], gout_ref[...])


def ffn_half_step(h, g_in, wg, wu, wd, g_out, tm=512, tf=512):
    m, d = h.shape
    f = wg.shape[1]
    tm = min(tm, m)
    tf = min(tf, f)
    return pl.pallas_call(
        _ffn_body,
        grid=(m // tm, f // tf),
        in_specs=[
            pl.BlockSpec((tm, d), lambda i, j: (i, 0)),
            pl.BlockSpec((1, d), lambda i, j: (0, 0)),
            pl.BlockSpec((d, tf), lambda i, j: (0, j)),
            pl.BlockSpec((d, tf), lambda i, j: (0, j)),
            pl.BlockSpec((tf, d), lambda i, j: (j, 0)),
            pl.BlockSpec((1, d), lambda i, j: (0, 0)),
        ],
        out_specs=pl.BlockSpec((tm, d), lambda i, j: (i, 0)),
        out_shape=jax.ShapeDtypeStruct((m, d), F32),
        scratch_shapes=[pltpu.VMEM((tm, d), BF16), pltpu.VMEM((tm, d), F32)],
        compiler_params=_cparams(("parallel", "arbitrary")),
        name="ffn_half_step",
    )(h, g_in.reshape(1, d), wg, wu, wd, g_out.reshape(1, d))


def _in_proj_body(x_ref, g_ref, w_ref, o_ref, xn_ref):
    @pl.when(pl.program_id(1) == 0)
    def _():
        xn_ref[...] = _rms(x_ref[...], g_ref[...]).astype(BF16)

    o_ref[...] = _dot(xn_ref[...], w_ref[...])


def norm_in_proj(h, g, w, tm=1024, tn=512):
    m, d = h.shape
    n = w.shape[1]
    tm = min(tm, m)
    tn = min(tn, n)
    return pl.pallas_call(
        _in_proj_body,
        grid=(m // tm, n // tn),
        in_specs=[
            pl.BlockSpec((tm, d), lambda i, j: (i, 0)),
            pl.BlockSpec((1, d), lambda i, j: (0, 0)),
            pl.BlockSpec((d, tn), lambda i, j: (0, j)),
        ],
        out_specs=pl.BlockSpec((tm, tn), lambda i, j: (i, j)),
        out_shape=jax.ShapeDtypeStruct((m, n), F32),
        scratch_shapes=[pltpu.VMEM((tm, d), BF16)],
        compiler_params=_cparams(("parallel", "arbitrary")),
        name="norm_in_proj",
    )(h, g.reshape(1, d), w)


def _out_proj_body(a_ref, b_ref, wa_ref, wb_ref, g_ref, h_ref, o_ref):
    m = _dot(a_ref[...], wa_ref[...]) + _dot(b_ref[...], wb_ref[...])
    o_ref[...] = h_ref[...] + _rms(m, g_ref[...])


def out_proj_residual(oa, ob, wa, wb, g, h, tm=256):
    m, d = h.shape
    ka, kb = oa.shape[1], ob.shape[1]
    tm = min(tm, m)
    return pl.pallas_call(
        _out_proj_body,
        grid=(m // tm,),
        in_specs=[
            pl.BlockSpec((tm, ka), lambda i: (i, 0)),
            pl.BlockSpec((tm, kb), lambda i: (i, 0)),
            pl.BlockSpec((ka, d), lambda i: (0, 0)),
            pl.BlockSpec((kb, d), lambda i: (0, 0)),
            pl.BlockSpec((1, d), lambda i: (0, 0)),
            pl.BlockSpec((tm, d), lambda i: (i, 0)),
        ],
        out_specs=pl.BlockSpec((tm, d), lambda i: (i, 0)),
        out_shape=jax.ShapeDtypeStruct((m, d), F32),
        compiler_params=_cparams(("parallel",)),
        name="out_proj_residual",
    )(oa, ob, wa, wb, g.reshape(1, d), h)


def _rel_bucket(dist):
    n = jnp.maximum(dist, 0)
    exact = REL_BUCKETS // 2
    nf = jnp.maximum(n, 1).astype(F32)
    large = exact + (jnp.log(nf / exact) / math.log(REL_MAX_DIST / exact) * (REL_BUCKETS - exact)).astype(jnp.int32)
    large = jnp.minimum(large, REL_BUCKETS - 1)
    return jnp.where(n < exact, n, large)


def _bias_of_dist(rel_bias, dist):
    return jnp.moveaxis(rel_bias[_rel_bucket(dist)], -1, 0)


def _ssd_body(xs_ref, bm_ref, cm_ref, z_ref, dt_ref, cwx_ref, cwb_ref, cwc_ref, cbx_ref, cbb_ref, cbc_ref,
              dtb_ref, alog_ref, dskip_ref, ng_ref, o_ref, cat_ref, state_ref):
    c = pl.program_id(2)
    L = xs_ref.shape[0]
    HG = SSD_HEADS // SSD_GROUPS
    P = SSD_HEAD_DIM
    WX = HG * P

    @pl.when(c == 0)
    def _():
        cat_ref[0:8, :] = jnp.zeros((8, cat_ref.shape[1]), F32)
        state_ref[...] = jnp.zeros_like(state_ref)

    @pl.when(c > 0)
    def _():
        cat_ref[0:8, :] = cat_ref[L:L + 8, :]

    cat_ref[8:L + 8, 0:WX] = xs_ref[...]
    cat_ref[8:L + 8, WX:WX + SSD_STATE] = bm_ref[...]
    cat_ref[8:L + 8, WX + SSD_STATE:WX + 2 * SSD_STATE] = cm_ref[...]

    def conv(lo, hi, w_ref, b_ref):
        acc = b_ref[...] + w_ref[SSD_CONV - 1:SSD_CONV, :] * cat_ref[8:L + 8, lo:hi]
        for k in range(SSD_CONV - 1):
            s = SSD_CONV - 1 - k
            acc = acc + w_ref[k:k + 1, :] * cat_ref[pl.ds(8 - s, L), lo:hi]
        return acc * jax.nn.sigmoid(acc)

    xs = conv(0, WX, cwx_ref, cbx_ref)
    bm = conv(WX, WX + SSD_STATE, cwb_ref, cbb_ref)
    cm = conv(WX + SSD_STATE, WX + 2 * SSD_STATE, cwc_ref, cbc_ref)

    dt = jax.nn.softplus(dt_ref[...] + dtb_ref[...])
    a = -jnp.exp(alog_ref[...])
    row = lax.broadcasted_iota(jnp.int32, (L, L), 0)
    col = lax.broadcasted_iota(jnp.int32, (L, L), 1)
    causal = col <= row
    tril = causal.astype(F32)
    acum = _dot_hi(tril, dt * a)
    acum_t = acum.T
    er = lax.broadcasted_iota(jnp.int32, (LANES, WX), 0)
    ec = lax.broadcasted_iota(jnp.int32, (LANES, WX), 1)
    expand = (er == ec // P).astype(F32)
    dt_e = _dot_hi(dt, expand)
    acum_e = _dot_hi(acum, expand)
    last_e = acum_e[L - 1:L, :]
    xdt = xs * dt_e
    cm_b = cm.astype(BF16)
    bm_b = bm.astype(BF16)
    cb = _dot_t(cm_b, bm_b)
    state = state_ref[...]
    y_inter = _dot(cm_b, state.astype(BF16)) * jnp.exp(acum_e)
    xdt_b = xdt.astype(BF16)
    parts = []
    for h in range(HG):
        seg = acum[:, h:h + 1] - acum_t[h:h + 1, :]
        dec = jnp.exp(jnp.where(causal, seg, NEG_INF))
        gmat = (cb * dec).astype(BF16)
        parts.append(_dot(gmat, xdt_b[:, h * P:(h + 1) * P]))
    y = jnp.concatenate(parts, axis=1) + y_inter
    xw = (xdt * jnp.exp(last_e - acum_e)).astype(BF16)
    state_ref[...] = state * jnp.exp(last_e) + _dot(bm.T.astype(BF16), xw)
    y = y + dskip_ref[...] * xs
    zz = z_ref[...]
    y = y * (zz * jax.nn.sigmoid(zz))
    o_ref[...] = _rms(y, ng_ref[...]).astype(o_ref.dtype)


def ssd_mixer(p, cols, batch, conv_w, conv_b, dt_bias, a_log, d_skip, norm_g):
    m = p.shape[0]
    t = m // batch
    L = SSD_CHUNK
    nc = t // L
    G = SSD_GROUPS
    HG = SSD_HEADS // G
    WX = HG * SSD_HEAD_DIM
    N = SSD_STATE
    inner = SSD_HEADS * SSD_HEAD_DIM
    zb, xb, db = cols["z"] // WX, cols["xbc"] // WX, cols["dt"] // LANES
    bb = (cols["xbc"] + inner) // N
    cb_ = (cols["xbc"] + inner + G * N) // N

    def rows(b, g, c):
        return b * nc + c

    pad8 = lambda v: jnp.pad(v.reshape(G, 1, HG), ((0, 0), (0, 0), (0, LANES - HG)))
    cw = conv_w.astype(F32)
    cbias = conv_b.astype(F32).reshape(1, -1)
    cwx, cwb, cwc = cw[:, :inner], cw[:, inner:inner + G * N], cw[:, inner + G * N:]
    cbx, cbb, cbc = cbias[:, :inner], cbias[:, inner:inner + G * N], cbias[:, inner + G * N:]
    dsk = jnp.repeat(d_skip.astype(F32), SSD_HEAD_DIM).reshape(G, 1, WX)
    return pl.pallas_call(
        _ssd_body,
        grid=(batch, G, nc),
        in_specs=[
            pl.BlockSpec((L, WX), lambda b, g, c: (rows(b, g, c), xb + g)),
            pl.BlockSpec((L, N), lambda b, g, c: (rows(b, g, c), bb + g)),
            pl.BlockSpec((L, N), lambda b, g, c: (rows(b, g, c), cb_ + g)),
            pl.BlockSpec((L, WX), lambda b, g, c: (rows(b, g, c), zb + g)),
            pl.BlockSpec((L, LANES), lambda b, g, c: (rows(b, g, c), db + g)),
            pl.BlockSpec((SSD_CONV, WX), lambda b, g, c: (0, g)),
            pl.BlockSpec((SSD_CONV, N), lambda b, g, c: (0, g)),
            pl.BlockSpec((SSD_CONV, N), lambda b, g, c: (0, g)),
            pl.BlockSpec((1, WX), lambda b, g, c: (0, g)),
            pl.BlockSpec((1, N), lambda b, g, c: (0, g)),
            pl.BlockSpec((1, N), lambda b, g, c: (0, g)),
            pl.BlockSpec((None, 1, LANES), lambda b, g, c: (g, 0, 0)),
            pl.BlockSpec((None, 1, LANES), lambda b, g, c: (g, 0, 0)),
            pl.BlockSpec((None, 1, WX), lambda b, g, c: (g, 0, 0)),
            pl.BlockSpec((1, WX), lambda b, g, c: (0, g)),
        ],
        out_specs=pl.BlockSpec((L, WX), lambda b, g, c: (rows(b, g, c), g)),
        out_shape=jax.ShapeDtypeStruct((m, inner), BF16),
        scratch_shapes=[pltpu.VMEM((L + 8, WX + 2 * N), F32), pltpu.VMEM((N, WX), F32)],
        compiler_params=_cparams(("parallel", "parallel", "arbitrary")),
        name="ssd_mixer",
    )(p, p, p, p, p, cwx, cwb, cwc, cbx, cbb, cbc, pad8(dt_bias.astype(F32)), pad8(a_log.astype(F32)), dsk,
      norm_g.astype(F32).reshape(1, inner))


def _nsa_compress_body(x_ref, pe_ref, w1_ref, w2_ref, o_ref):
    x = x_ref[...]
    nr = x.shape[0]
    p0 = _dot((x + pe_ref[0:1, :]).astype(BF16), w1_ref[0])
    p1 = _dot((x + pe_ref[1:2, :]).astype(BF16), w1_ref[1])
    pre = p0 + pltpu.roll(p1, shift=nr - 1, axis=0)
    hdn = (pre * jax.nn.sigmoid(pre)).astype(BF16)
    out = _dot(hdn, w2_ref[...])
    valid = lax.broadcasted_iota(jnp.int32, out.shape, 0) < nr - 1
    o_ref[...] = jnp.where(valid, out, 0.0).astype(o_ref.dtype)


def nsa_compress(x, pe, w1, w2):
    b, _, g, nr, dk = x.shape
    hid = w1.shape[-1]
    dh = w2.shape[-1]
    return pl.pallas_call(
        _nsa_compress_body,
        grid=(b, 2, g),
        in_specs=[
            pl.BlockSpec((None, None, None, nr, dk), lambda i, s, j: (i, s, j, 0, 0)),
            pl.BlockSpec((None, 2, dk), lambda i, s, j: (s, 0, 0)),
            pl.BlockSpec((None, 2, dk, hid), lambda i, s, j: (s, 0, 0, 0)),
            pl.BlockSpec((None, hid, dh), lambda i, s, j: (s, 0, 0)),
        ],
        out_specs=pl.BlockSpec((None, None, None, nr, dh), lambda i, s, j: (i, s, j, 0, 0)),
        out_shape=jax.ShapeDtypeStruct((b, 2, g, nr, dh), BF16),
        compiler_params=_cparams(("parallel", "parallel", "parallel")),
        name="nsa_compress",
    )(x, pe, w1, w2)


NSA_CMP_PAD = 16
NSA_SEL_PAD = 128
NSA_FAR_TILE = 512
NSA_NEAR = 256


def _softmax_parts(parts):
    sm = [jnp.where(m, s, NEG_INF) for s, m in parts]
    mx = functools.reduce(jnp.maximum, [jnp.max(s, axis=-1, keepdims=True) for s in sm])
    es = [jnp.where(m, jnp.exp(s - mx), 0.0) for s, (_, m) in zip(sm, parts)]
    l = functools.reduce(jnp.add, [jnp.sum(e, axis=-1, keepdims=True) for e in es])
    inv = jnp.where(l > 0.0, 1.0 / l, 0.0)
    return [e * inv for e in es]


def _nsa_body(cfar_ref, q_ref, gate_ref, kc_ref, vc_ref, ks_ref, vs_ref, kw_ref, vw_ref,
              tc_ref, ts_ref, tw_ref, ov_ref, o_ref, st_ref, *, n_top):
    g = pl.program_id(1)
    qb = pl.program_id(2)
    Q = Q_BLOCK
    R = NSA_HEADS // NSA_KV_GROUPS
    Dh = NSA_HEAD_DIM
    t0 = qb * Q
    nck = kc_ref.shape[0] - 128

    def rep(x):
        return jnp.concatenate([x] * R, axis=0)

    q = q_ref[...] * (Dh ** -0.5)
    qs = jnp.concatenate([q[:, r * Dh:(r + 1) * Dh] for r in range(R)], axis=0).astype(BF16)
    cfar = jnp.concatenate([jnp.full((Q, 1), cfar_ref[g * R + r], F32) for r in range(R)], axis=0)
    i_q = lax.broadcasted_iota(jnp.int32, (Q, 1), 0)

    ncmp = nck - NSA_CMP_PAD
    kc_all = kc_ref[NSA_CMP_PAD:nck, :]
    vc_all = vc_ref[NSA_CMP_PAD:nck, :]
    n_far = lax.broadcasted_iota(jnp.int32, (1, ncmp), 1)
    mask_far = rep(jnp.broadcast_to(n_far < 8 * qb - NSA_CMP_PAD, (Q, ncmp)))
    s_far = _dot_t(qs, kc_all) + cfar
    near0 = pl.multiple_of(8 * qb, 8)
    kc_near = kc_ref[pl.ds(near0, 128), :]
    vc_near = vc_ref[pl.ds(near0, 128), :]
    m_near = lax.broadcasted_iota(jnp.int32, (1, 128), 1)
    dist_c = i_q - NSA_CMP_STRIDE * (m_near - NSA_CMP_PAD) - (NSA_CMP_LEN - 1)
    mask_near = rep((dist_c >= 0) & (m_near + 8 * qb - NSA_CMP_PAD >= 0) & (m_near < 2 * NSA_CMP_PAD))
    s_near = _dot_t(qs, kc_near) + tc_ref[...].reshape(R * Q, 128)
    p_far, p_near = _softmax_parts([(s_far, mask_far), (s_near, mask_near)])
    p_far = p_far.astype(BF16)
    p_near = p_near.astype(BF16)
    o_c = _dot(p_far, vc_all) + _dot(p_near, vc_near)
    imp4 = _dot(p_far, ov_ref[NSA_CMP_PAD:nck, :]) + _dot(p_near, ov_ref[pl.ds(near0, 128), :])
    imp = functools.reduce(jnp.add, [imp4[r * Q:(r + 1) * Q] for r in range(R)])

    j_blk = lax.broadcasted_iota(jnp.int32, (1, 128), 1)
    t_q = t0 + i_q
    cur = 2 * qb + (i_q >= NSA_SEL_LEN).astype(jnp.int32)
    forced = (j_blk == 0) | (j_blk == cur) | (j_blk == cur - 1)
    adm = j_blk * NSA_SEL_LEN <= t_q
    score = jnp.where(adm, imp + NSA_FORCE * forced.astype(F32), NEG_INF)
    st_ref[...] = score.T
    st = st_ref[...]
    j_row = lax.broadcasted_iota(jnp.int32, (128, Q), 0)

    def rank_body(jp, cnt):
        row = st_ref[pl.ds(jp, 1), :]
        ahead = (row > st) | ((row == st) & (jp < j_row))
        return cnt + ahead.astype(jnp.int32)

    cnt = lax.fori_loop(0, 128, rank_body, jnp.zeros((128, Q), jnp.int32), unroll=8)
    sel_b = (cnt < n_top).astype(F32).T.astype(BF16)

    jr = lax.broadcasted_iota(jnp.int32, (128, NSA_FAR_TILE), 0)
    jc = lax.broadcasted_iota(jnp.int32, (128, NSA_FAR_TILE), 1)
    blk_delta = jr - jc // NSA_SEL_LEN
    c_far = lax.broadcasted_iota(jnp.int32, (1, NSA_FAR_TILE), 1)

    def far_body(kt, carry):
        m_i, l_i, acc = carry
        k0 = pl.multiple_of(kt * NSA_FAR_TILE, NSA_FAR_TILE)
        k = ks_ref[pl.ds(NSA_SEL_PAD + k0, NSA_FAR_TILE), :]
        v = vs_ref[pl.ds(NSA_SEL_PAD + k0, NSA_FAR_TILE), :]
        expand = (blk_delta == kt * (NSA_FAR_TILE // NSA_SEL_LEN)).astype(BF16)
        member = _dot(sel_b, expand) > 0.5
        mask = rep(member & (k0 + c_far < t0 - NSA_SEL_PAD))
        s = jnp.where(mask, _dot_t(qs, k) + cfar, NEG_INF)
        m_new = jnp.maximum(m_i, jnp.max(s, axis=-1, keepdims=True))
        p = jnp.where(mask, jnp.exp(s - m_new), 0.0)
        alpha = jnp.exp(m_i - m_new)
        l_new = alpha * l_i + jnp.sum(p, axis=-1, keepdims=True)
        acc_new = alpha * acc + _dot(p.astype(BF16), v)
        return m_new, l_new, acc_new

    n_far_tiles = (qb + 2) // 4
    init = (jnp.full((R * Q, 1), NEG_INF, F32), jnp.zeros((R * Q, 1), F32), jnp.zeros((R * Q, Dh), F32))
    m_i, l_i, acc = lax.fori_loop(0, n_far_tiles, far_body, init)

    t0a = pl.multiple_of(t0, Q)
    k = ks_ref[pl.ds(t0a, NSA_NEAR), :]
    v = vs_ref[pl.ds(t0a, NSA_NEAR), :]
    c_near = lax.broadcasted_iota(jnp.int32, (1, NSA_NEAR), 1)
    pick = (blk_delta[:, :NSA_NEAR] == 2 * qb - NSA_SEL_PAD // NSA_SEL_LEN).astype(BF16)
    member = _dot(sel_b, pick) > 0.5
    mask = rep(member & (c_near <= i_q + NSA_SEL_PAD) & (c_near >= NSA_SEL_PAD - t0))
    s = jnp.where(mask, _dot_t(qs, k) + ts_ref[...].reshape(R * Q, NSA_NEAR), NEG_INF)
    m_new = jnp.maximum(m_i, jnp.max(s, axis=-1, keepdims=True))
    p = jnp.where(mask, jnp.exp(s - m_new), 0.0)
    alpha = jnp.exp(m_i - m_new)
    l_s = alpha * l_i + jnp.sum(p, axis=-1, keepdims=True)
    o_s = (alpha * acc + _dot(p.astype(BF16), v)) * jnp.where(l_s > 0.0, 1.0 / l_s, 0.0)

    wlen = NSA_WINDOW + Q
    kwin = kw_ref[pl.ds(t0a, wlen), :]
    vwin = vw_ref[pl.ds(t0a, wlen), :]
    c_w = lax.broadcasted_iota(jnp.int32, (1, wlen), 1)
    dist_w = i_q + NSA_WINDOW - c_w
    mask_w = rep((dist_w >= 0) & (dist_w < NSA_WINDOW) & (c_w >= NSA_WINDOW - t0))
    s_w = _dot_t(qs, kwin) + tw_ref[...].reshape(R * Q, wlen)
    (p_w,) = _softmax_parts([(s_w, mask_w)])
    o_w = _dot(p_w.astype(BF16), vwin)

    gates = jax.nn.sigmoid(gate_ref[...])
    outs = []
    for r in range(R):
        rs = slice(r * Q, (r + 1) * Q)
        outs.append(gates[:, 3 * r:3 * r + 1] * o_c[rs] + gates[:, 3 * r + 1:3 * r + 2] * o_s[rs]
                    + gates[:, 3 * r + 2:3 * r + 3] * o_w[rs])
    o_ref[...] = jnp.concatenate(outs, axis=1).astype(o_ref.dtype)


def nsa_attention(p, cols, batch, cmp_kv, sel_k, sel_v, win_k, win_v, rel_bias):
    m = p.shape[0]
    t = m // batch
    G = NSA_KV_GROUPS
    R = NSA_HEADS // G
    Dh = NSA_HEAD_DIM
    Q = Q_BLOCK
    nq = t // Q
    ncmp = t // NSA_CMP_STRIDE
    n_sel = t // NSA_SEL_LEN
    assert n_sel <= 128 and t % NSA_FAR_TILE == 0
    n_top = min(NSA_SEL_BLOCKS, n_sel)
    wlen = NSA_WINDOW + Q

    cmp_p = jnp.pad(cmp_kv, ((0, 0), (0, 0), (0, 0), (NSA_CMP_PAD, 128), (0, 0)))
    pad_t = lambda a, n: jnp.pad(a, ((0, 0), (0, 0), (n, 0), (0, 0)))
    ks_p, vs_p = pad_t(sel_k, NSA_SEL_PAD), pad_t(sel_v, NSA_SEL_PAD)
    kw_p, vw_p = pad_t(win_k, NSA_WINDOW), pad_t(win_v, NSA_WINDOW)

    i_q = jnp.arange(Q)[:, None]
    rb = rel_bias.astype(F32)
    tc = _bias_of_dist(rb, i_q - NSA_CMP_STRIDE * (jnp.arange(128)[None, :] - NSA_CMP_PAD) - (NSA_CMP_LEN - 1))
    ts = _bias_of_dist(rb, i_q + NSA_SEL_PAD - jnp.arange(NSA_NEAR)[None, :])
    tw = _bias_of_dist(rb, i_q + NSA_WINDOW - jnp.arange(wlen)[None, :])
    cfar = rb[REL_BUCKETS - 1]
    n_idx = np.arange(ncmp + NSA_CMP_PAD + 128)[:, None] - NSA_CMP_PAD
    j_idx = np.arange(128)[None, :]
    ov = ((n_idx >= 0) & (n_idx < ncmp - 1) & (n_idx * NSA_CMP_STRIDE < (j_idx + 1) * NSA_SEL_LEN)
          & (n_idx * NSA_CMP_STRIDE + NSA_CMP_LEN - 1 >= j_idx * NSA_SEL_LEN))
    ov = jnp.asarray(ov, BF16)

    qblk = cols["q"] // (R * Dh)
    gblk = cols["gates"] // LANES
    kv_spec = lambda rows_: pl.BlockSpec((None, None, rows_, Dh), lambda b, g, i: (b, g, 0, 0))
    return pl.pallas_call(
        functools.partial(_nsa_body, n_top=n_top),
        grid=(batch, G, nq),
        in_specs=[
            pl.BlockSpec(memory_space=pltpu.SMEM),
            pl.BlockSpec((Q, R * Dh), lambda b, g, i: (b * nq + i, qblk + g)),
            pl.BlockSpec((Q, LANES), lambda b, g, i: (b * nq + i, gblk + g)),
            pl.BlockSpec((None, None, None, ncmp + NSA_CMP_PAD + 128, Dh), lambda b, g, i: (b, 0, g, 0, 0)),
            pl.BlockSpec((None, None, None, ncmp + NSA_CMP_PAD + 128, Dh), lambda b, g, i: (b, 1, g, 0, 0)),
            kv_spec(t + NSA_SEL_PAD), kv_spec(t + NSA_SEL_PAD),
            kv_spec(t + NSA_WINDOW), kv_spec(t + NSA_WINDOW),
            pl.BlockSpec((R, Q, 128), lambda b, g, i: (g, 0, 0)),
            pl.BlockSpec((R, Q, NSA_NEAR), lambda b, g, i: (g, 0, 0)),
            pl.BlockSpec((R, Q, wlen), lambda b, g, i: (g, 0, 0)),
            pl.BlockSpec((ncmp + NSA_CMP_PAD + 128, 128), lambda b, g, i: (0, 0)),
        ],
        out_specs=pl.BlockSpec((Q, R * Dh), lambda b, g, i: (b * nq + i, g)),
        out_shape=jax.ShapeDtypeStruct((m, NSA_HEADS * Dh), BF16),
        scratch_shapes=[pltpu.VMEM((128, Q), F32)],
        compiler_params=_cparams(("parallel", "parallel", "arbitrary")),
        name="nsa_attention",
    )(cfar, p, p, cmp_p, cmp_p, ks_p, vs_p, kw_p, vw_p, tc, ts, tw, ov)


def _dsa_prep_body(cq_ref, ckv_ref, kidx_ref, widx_ref, gq_ref, gkv_ref, wuq_ref, wuk_ref, wqi_ref,
                   qa_ref, qi_ref, ckvn_ref, ka_ref, kb_ref, w_ref):
    H, Dh = DSA_HEADS, DSA_HEAD_DIM
    cq = _rms(cq_ref[...], gq_ref[...]).astype(BF16)
    ckvn_ref[...] = _rms(ckv_ref[...], gkv_ref[...]).astype(BF16)
    q = _dot(cq, wuq_ref[...]).astype(BF16)
    for h in range(H):
        qa = _dot(q[:, h * Dh:(h + 1) * Dh], wuk_ref[h]) * (Dh ** -0.5)
        qa_ref[:, h * DSA_KV_RANK:(h + 1) * DSA_KV_RANK] = qa.astype(BF16)
    qi_ref[...] = _dot(cq, wqi_ref[...]).astype(BF16)
    kidx = kidx_ref[...]
    lane = lax.broadcasted_iota(jnp.int32, kidx.shape, 1)
    ka_ref[...] = jnp.where(lane < DSA_IDX_DIM, kidx, 0.0).astype(BF16)
    kb_ref[...] = jnp.where(lane >= DSA_IDX_DIM, pltpu.roll(kidx, shift=DSA_IDX_DIM, axis=1), 0.0).astype(BF16)
    w_ref[...] = widx_ref[...] * (DSA_IDX_HEADS ** -0.5 * DSA_IDX_DIM ** -0.5)


def dsa_prep(p, cols, gq, gkv, wuq, wuk, wqi, tm=512):
    m = p.shape[0]
    tm = min(tm, m)
    H, Dh, C = DSA_HEADS, DSA_HEAD_DIM, DSA_KV_RANK
    full = lambda a: pl.BlockSpec(a.shape, lambda i: (0,) * a.ndim)
    gq2, gkv2 = gq.reshape(1, -1), gkv.reshape(1, -1)
    outs = [
        jax.ShapeDtypeStruct((m, H * C), BF16),
        jax.ShapeDtypeStruct((m, DSA_IDX_HEADS * DSA_IDX_DIM), BF16),
        jax.ShapeDtypeStruct((m, C), BF16),
        jax.ShapeDtypeStruct((m, LANES), BF16),
        jax.ShapeDtypeStruct((m, LANES), BF16),
        jax.ShapeDtypeStruct((m, LANES), F32),
    ]
    return pl.pallas_call(
        _dsa_prep_body,
        grid=(m // tm,),
        in_specs=[
            pl.BlockSpec((tm, DSA_Q_RANK), lambda i: (i, cols["cq"] // DSA_Q_RANK)),
            pl.BlockSpec((tm, C), lambda i: (i, cols["ckv"] // C)),
            pl.BlockSpec((tm, LANES), lambda i: (i, cols["kidx"] // LANES)),
            pl.BlockSpec((tm, LANES), lambda i: (i, cols["widx"] // LANES)),
            full(gq2), full(gkv2), full(wuq), full(wuk), full(wqi),
        ],
        out_specs=[pl.BlockSpec((tm, o.shape[1]), lambda i: (i, 0)) for o in outs],
        out_shape=outs,
        compiler_params=_cparams(("parallel",)),
        name="dsa_prep",
    )(p, p, p, p, gq2, gkv2, wuq, wuk, wqi)


DSA_TILE = 512
DSA_CHUNK = 128


def _key_to_f32(key):
    bits = key ^ ((key >> 31) & jnp.int32(0x7FFFFFFF))
    return lax.bitcast_convert_type(bits, F32)


def _dsa_body(cfar_ref, qi_ref, w_ref, qa_ref, ka_ref, kb_ref, ckv_ref, tb_ref, wuv_ref, o_ref,
              sc_ref, m_ref, l_ref, acc_ref, *, n_keep, idx_bits):
    qb = pl.program_id(1)
    Q = Q_BLOCK
    H = DSA_HEADS
    C = DSA_KV_RANK
    TK = DSA_TILE
    CPT = TK // DSA_CHUNK
    t0 = qb * Q
    i_q = lax.broadcasted_iota(jnp.int32, (Q, 1), 0)
    t_q = t0 + i_q
    n_tiles = qb // CPT + 1

    sc_ref[0] = jnp.full((Q, DSA_CHUNK), -jnp.inf, F32)
    qi = qi_ref[...]
    npair = DSA_IDX_HEADS // 2
    qstack = jnp.concatenate([qi[:, j * LANES:(j + 1) * LANES] for j in range(npair)], axis=0)
    w = w_ref[...]
    c_tile = lax.broadcasted_iota(jnp.int32, (1, TK), 1)

    def idx_body(kt, carry):
        k0 = pl.multiple_of(kt * TK, TK)
        se = _dot_t(qstack, ka_ref[pl.ds(DSA_CHUNK + k0, TK), :])
        so = _dot_t(qstack, kb_ref[pl.ds(DSA_CHUNK + k0, TK), :])
        tot = jnp.zeros((Q, TK), F32)
        for j in range(npair):
            rs = slice(j * Q, (j + 1) * Q)
            tot = tot + jnp.maximum(se[rs], 0.0) * w[:, 2 * j:2 * j + 1]
            tot = tot + jnp.maximum(so[rs], 0.0) * w[:, 2 * j + 1:2 * j + 2]
        tot = jnp.where(k0 + c_tile <= t_q, tot + 0.0, -jnp.inf)
        for u in range(CPT):
            sc_ref[1 + kt * CPT + u] = tot[:, u * DSA_CHUNK:(u + 1) * DSA_CHUNK]
        return carry

    lax.fori_loop(0, n_tiles, idx_body, 0)

    def tile_scores(kt):
        blk = sc_ref[pl.ds(1 + kt * CPT, CPT)]
        return jnp.concatenate([blk[u] for u in range(CPT)], axis=1)

    def count(pred):
        def body(kt, acc):
            return acc + pred(tile_scores(kt), kt).astype(F32)
        acc = lax.fori_loop(0, n_tiles, body, jnp.zeros((Q, TK), F32))
        return jnp.sum(acc, axis=-1, keepdims=True)

    keep = jnp.float32(n_keep)

    def bis_body(it, key):
        cand = key + jnp.left_shift(jnp.int32(1), 31 - it)
        cf = _key_to_f32(cand)
        ok = count(lambda x, kt: x >= cf) >= keep
        return jnp.where(ok, cand, key)

    key = lax.fori_loop(0, 32, bis_body, jnp.full((Q, 1), jnp.iinfo(jnp.int32).min, jnp.int32))
    few = t_q < n_keep
    thr = jnp.where(few, -jnp.inf, _key_to_f32(key))
    n_gt = count(lambda x, kt: x > thr)
    n_ge = count(lambda x, kt: x >= thr)
    need = keep - n_gt
    tied = jnp.max(jnp.where(few, 0.0, n_ge - n_gt - need)) > 0.0

    def tie_cut(_):
        def body(it, qcut):
            cand = qcut + jnp.left_shift(jnp.int32(1), idx_bits - 1 - it)
            n = count(lambda x, kt: (x == thr) & (kt * TK + c_tile < cand))
            return jnp.where(n < need, cand, qcut)
        return lax.fori_loop(0, idx_bits, body, jnp.zeros((Q, 1), jnp.int32))

    big = jnp.full((Q, 1), jnp.iinfo(jnp.int32).max, jnp.int32)
    qcut = lax.cond(tied, tie_cut, lambda _: big, 0)
    qcut = jnp.where(few, big, qcut)

    def member_of(x, s_pos):
        return (x > thr) | ((x == thr) & (s_pos <= qcut))

    HG = 4
    c_near = lax.broadcasted_iota(jnp.int32, (1, 2 * DSA_CHUNK), 1)
    outs = []
    for hg in range(H // HG):
        rep = lambda x: jnp.concatenate([x] * HG, axis=0)
        qa = jnp.concatenate([qa_ref[:, (hg * HG + r) * C:(hg * HG + r + 1) * C] for r in range(HG)], axis=0)
        cfar = jnp.concatenate([jnp.full((Q, 1), cfar_ref[hg * HG + r], F32) for r in range(HG)], axis=0)
        m_ref[...] = jnp.full(m_ref.shape, NEG_INF, F32)
        l_ref[...] = jnp.zeros(l_ref.shape, F32)
        acc_ref[...] = jnp.zeros(acc_ref.shape, F32)

        def step(s, mask, kv):
            s = jnp.where(mask, s, NEG_INF)
            m_i = m_ref[...]
            m_new = jnp.maximum(m_i, jnp.max(s, axis=-1, keepdims=True))
            p = jnp.where(mask, jnp.exp(s - m_new), 0.0)
            alpha = jnp.exp(m_i - m_new)
            l_ref[...] = alpha * l_ref[...] + jnp.sum(p, axis=-1, keepdims=True)
            acc_ref[...] = alpha * acc_ref[...] + _dot(p.astype(BF16), kv)
            m_ref[...] = m_new

        def far_body(kt, carry):
            k0 = pl.multiple_of(kt * TK, TK)
            kv = ckv_ref[pl.ds(DSA_CHUNK + k0, TK), :]
            s_pos = k0 + c_tile
            mask = rep(member_of(tile_scores(kt), s_pos) & (s_pos < t0 - DSA_CHUNK))
            step(_dot_t(qa, kv) + cfar, mask, kv)
            return carry

        lax.fori_loop(0, (qb + 2) // CPT, far_body, 0)
        t0a = pl.multiple_of(t0, Q)
        kv = ckv_ref[pl.ds(t0a, 2 * DSA_CHUNK), :]
        blk = sc_ref[pl.ds(qb, 2)]
        x = jnp.concatenate([blk[0], blk[1]], axis=1)
        s_pos = t0 - DSA_CHUNK + c_near
        mask = rep(member_of(x, s_pos) & (s_pos <= t_q) & (s_pos >= 0))
        step(_dot_t(qa, kv) + tb_ref[hg * HG:(hg + 1) * HG].reshape(HG * Q, 2 * DSA_CHUNK), mask, kv)
        l = l_ref[...]
        o = (acc_ref[...] * jnp.where(l > 0.0, 1.0 / l, 0.0)).astype(BF16)
        for r in range(HG):
            outs.append(_dot(o[r * Q:(r + 1) * Q], wuv_ref[hg * HG + r]))
    o_ref[...] = jnp.concatenate(outs, axis=1).astype(o_ref.dtype)


def dsa_attention(qa, qi, ckvn, ka, kb, widx, batch, rel_bias, wuv):
    m = qa.shape[0]
    t = m // batch
    Q = Q_BLOCK
    nq = t // Q
    H, C, Dh = DSA_HEADS, DSA_KV_RANK, DSA_HEAD_DIM
    assert t % DSA_TILE == 0
    n_keep = min(DSA_TOPK, t // 4)
    idx_bits = int(t).bit_length()
    padk = lambda a: jnp.pad(a.reshape(batch, t, -1), ((0, 0), (DSA_CHUNK, 0), (0, 0)))
    ka_p, kb_p, ckv_p = padk(ka), padk(kb), padk(ckvn)
    rb = rel_bias.astype(F32)
    tb = _bias_of_dist(rb, jnp.arange(Q)[:, None] + DSA_CHUNK - jnp.arange(2 * DSA_CHUNK)[None, :])
    cfar = rb[REL_BUCKETS - 1]
    HG = 4
    return pl.pallas_call(
        functools.partial(_dsa_body, n_keep=n_keep, idx_bits=idx_bits),
        grid=(batch, nq),
        in_specs=[
            pl.BlockSpec(memory_space=pltpu.SMEM),
            pl.BlockSpec((Q, qi.shape[1]), lambda b, i: (b * nq + i, 0)),
            pl.BlockSpec((Q, LANES), lambda b, i: (b * nq + i, 0)),
            pl.BlockSpec((Q, H * C), lambda b, i: (b * nq + i, 0)),
            pl.BlockSpec((None, t + DSA_CHUNK, LANES), lambda b, i: (b, 0, 0)),
            pl.BlockSpec((None, t + DSA_CHUNK, LANES), lambda b, i: (b, 0, 0)),
            pl.BlockSpec((None, t + DSA_CHUNK, C), lambda b, i: (b, 0, 0)),
            pl.BlockSpec((H, Q, 2 * DSA_CHUNK), lambda b, i: (0, 0, 0)),
            pl.BlockSpec((H, C, Dh), lambda b, i: (0, 0, 0)),
        ],
        out_specs=pl.BlockSpec((Q, H * Dh), lambda b, i: (b * nq + i, 0)),
        out_shape=jax.ShapeDtypeStruct((m, H * Dh), BF16),
        scratch_shapes=[
            pltpu.VMEM((t // DSA_CHUNK + 1, Q, DSA_CHUNK), F32),
            pltpu.VMEM((HG * Q, 1), F32),
            pltpu.VMEM((HG * Q, 1), F32),
            pltpu.VMEM((HG * Q, C), F32),
        ],
        compiler_params=_cparams(("parallel", "arbitrary")),
        name="dsa_attention",
    )(cfar, qi, widx, qa, ka_p, kb_p, ckv_p, tb, wuv)


def _head_sum(x, hd):
    r = lax.broadcasted_iota(jnp.int32, (LANES, LANES), 0) // hd
    c = lax.broadcasted_iota(jnp.int32, (LANES, LANES), 1) // hd
    seg = (r == c).astype(F32)
    n = x.shape[1] // LANES
    return jnp.concatenate([_dot_hi(x[:, j * LANES:(j + 1) * LANES], seg) for j in range(n)], axis=1)


def _rwkv_prep_body(x_ref, prev_ref, mu_ref, w0_ref, w2_ref, a0_ref, a2_ref, g2_ref, kk_ref, ka_ref,
                    r_ref, lw_ref, k_ref, v_ref, a_ref, b_ref, g_ref, *, rows_per_seq):
    i = pl.program_id(0)
    C = RWKV_HEADS * RWKV_HEAD_DIM
    x = x_ref[...]
    tm = x.shape[0]
    first = (i * tm) % rows_per_seq == 0
    prev_row = jnp.where(first, 0.0, prev_ref[7:8, :])
    rolled = pltpu.roll(x, shift=1, axis=0)
    prev = jnp.where(lax.broadcasted_iota(jnp.int32, x.shape, 0) == 0, prev_row, rolled)
    x = x + (prev - x) * mu_ref[...]
    r, k, v = x[:, 0:C], x[:, C:2 * C], x[:, 2 * C:3 * C]
    wl = x[:, 3 * C:3 * C + RWKV_W_LORA]
    al = x[:, 3 * C + RWKV_W_LORA:3 * C + LANES]
    gl = x[:, 3 * C + LANES:3 * C + LANES + RWKV_G_LORA]
    w = -jax.nn.softplus(-(w0_ref[...] + _dot(jnp.tanh(wl).astype(BF16), w2_ref[...]))) - 0.5
    eta = jax.nn.sigmoid(a0_ref[...] + _dot(al.astype(BF16), a2_ref[...]))
    g_ref[...] = _dot(jax.nn.sigmoid(gl).astype(BF16), g2_ref[...])
    kk = k * kk_ref[...]
    kk = kk / jnp.maximum(jnp.sqrt(_head_sum(kk * kk, RWKV_HEAD_DIM)), 1e-12)
    r_ref[...] = r
    lw_ref[...] = -jnp.exp(w)
    k_ref[...] = k * (1.0 + (eta - 1.0) * ka_ref[...])
    v_ref[...] = v
    a_ref[...] = -kk
    b_ref[...] = kk * eta


def rwkv_prep(p, col, seq_len, mu, w0, w2, a0, a2, g2, k_k, k_a, tm=256):
    m = p.shape[0]
    C = RWKV_HEADS * RWKV_HEAD_DIM
    width = 3 * C + LANES + RWKV_G_LORA
    tm = min(tm, m)
    assert col % LANES == 0 and seq_len % tm == 0
    x = lax.slice_in_dim(p, col, col + width, axis=1)
    row = lambda a: a.astype(F32).reshape(1, -1)
    full = lambda a: pl.BlockSpec(a.shape, lambda i: (0,) * a.ndim)
    args = [row(mu), row(w0), w2.astype(BF16), row(a0), a2.astype(BF16), g2.astype(BF16), row(k_k), row(k_a)]
    out = jax.ShapeDtypeStruct((m, C), F32)
    return pl.pallas_call(
        functools.partial(_rwkv_prep_body, rows_per_seq=seq_len),
        grid=(m // tm,),
        in_specs=[pl.BlockSpec((tm, width), lambda i: (i, 0)),
                  pl.BlockSpec((8, width), lambda i: (jnp.maximum(i * (tm // 8) - 1, 0), 0))]
                 + [full(a) for a in args],
        out_specs=[pl.BlockSpec((tm, C), lambda i: (i, 0))] * 7,
        out_shape=[out] * 7,
        compiler_params=_cparams(("parallel",)),
        name="rwkv_prep",
    )(x, x, *args)


RWKV_HB = 4


def _rwkv_scan_body(r_ref, lw_ref, k_ref, v_ref, a_ref, b_ref, lwt_ref, kt_ref, bt_ref, y_ref, st_ref):
    c = pl.program_id(2)
    L = r_ref.shape[0]
    N = RWKV_HEAD_DIM

    @pl.when(c == 0)
    def _():
        st_ref[...] = jnp.zeros_like(st_ref)

    ri = lax.broadcasted_iota(jnp.int32, (L, L), 0)
    ci = lax.broadcasted_iota(jnp.int32, (L, L), 1)
    lower = ci <= ri
    strict = ci < ri
    lw = lw_ref[...]
    g = _dot_hi(lower.astype(F32), lw)
    eg = jnp.exp(g)
    ieg = jnp.exp(-g)
    at = a_ref[...] * jnp.exp(g - lw)
    rt = r_ref[...] * eg
    bt = b_ref[...] * ieg
    kt = k_ref[...] * ieg
    vv = v_ref[...]
    lwt = lwt_ref[...]
    gt = _dot_hi(lwt, (ri <= ci).astype(F32))
    wend = jnp.exp(gt[:, L - 1:L] - gt)
    bh_t = bt_ref[...] * wend
    kh_t = kt_ref[...] * wend
    dec_end = jnp.exp(gt[:, L - 1:L])
    eye = (ri == ci).astype(F32)
    ys = []
    for h in range(RWKV_HB):
        ls = slice(h * N, (h + 1) * N)
        lhs = jnp.concatenate([at[:, ls], rt[:, ls]], axis=0).astype(BF16)
        rhs = jnp.concatenate([bt[:, ls], kt[:, ls]], axis=0).astype(BF16)
        aa = _dot_t(lhs, rhs)
        a_ab = jnp.where(strict, aa[0:L, 0:L], 0.0)
        a_ak = jnp.where(strict, aa[0:L, L:2 * L], 0.0)
        m_rb = jnp.where(lower, aa[L:2 * L, 0:L], 0.0)
        m_rk = jnp.where(lower, aa[L:2 * L, L:2 * L], 0.0)
        tinv = eye + a_ab
        pw = a_ab
        n_sq = max(int(L - 1).bit_length() - 1, 0)
        for _ in range(n_sq):
            pw_b = pw.astype(BF16)
            pw = _dot(pw_b, pw_b)
            tinv = tinv + _dot(tinv.astype(BF16), pw.astype(BF16))
        st = st_ref[h]
        st_b = st.astype(BF16)
        v_b = vv[:, ls].astype(BF16)
        x = _dot(at[:, ls].astype(BF16), st_b) + _dot(a_ak.astype(BF16), v_b)
        u = _dot(tinv.astype(BF16), x.astype(BF16))
        u_b = u.astype(BF16)
        ys.append(_dot(rt[:, ls].astype(BF16), st_b) + _dot(m_rb.astype(BF16), u_b) + _dot(m_rk.astype(BF16), v_b))
        st_ref[h] = (dec_end[ls] * st + _dot(bh_t[ls].astype(BF16), u_b) + _dot(kh_t[ls].astype(BF16), v_b))
    y_ref[...] = jnp.concatenate(ys, axis=1)


def rwkv_scan(r, lw, k, v, a, b, batch):
    m, C = r.shape
    t = m // batch
    L = RWKV_CHUNK
    nc = t // L
    W = RWKV_HB * RWKV_HEAD_DIM
    tmaj = pl.BlockSpec((L, W), lambda bi, h, c: (bi * nc + c, h))
    cmaj = pl.BlockSpec((None, None, W, L), lambda bi, h, c: (bi, c, h, 0))
    chan = lambda z: z.reshape(batch, nc, L, C).transpose(0, 1, 3, 2)
    return pl.pallas_call(
        _rwkv_scan_body,
        grid=(batch, C // W, nc),
        in_specs=[tmaj] * 6 + [cmaj] * 3,
        out_specs=tmaj,
        out_shape=jax.ShapeDtypeStruct((m, C), F32),
        scratch_shapes=[pltpu.VMEM((RWKV_HB, RWKV_HEAD_DIM, RWKV_HEAD_DIM), F32)],
        compiler_params=_cparams(("parallel", "parallel", "arbitrary")),
        name="rwkv_scan",
    )(r, lw, k, v, a, b, chan(lw), chan(k), chan(b))


def _rwkv_post_body(y_ref, r_ref, k_ref, v_ref, g_ref, rk_ref, lng_ref, lnb_ref, o_ref):
    N = RWKV_HEAD_DIM
    y = y_ref[...]
    mean = _head_sum(y, N) * (1.0 / N)
    yc = y - mean
    var = _head_sum(yc * yc, N) * (1.0 / N)
    yn = yc * lax.rsqrt(var + RWKV_LN_EPS) * lng_ref[...] + lnb_ref[...]
    bonus = _head_sum(r_ref[...] * k_ref[...] * rk_ref[...], N) * v_ref[...]
    o_ref[...] = ((yn + bonus) * g_ref[...]).astype(o_ref.dtype)


def rwkv_post(y, r, k, v, g, r_k, ln_g, ln_b, tm=256):
    m, C = y.shape
    tm = min(tm, m)
    row = lambda a: a.astype(F32).reshape(1, C)
    blk = pl.BlockSpec((tm, C), lambda i: (i, 0))
    par = pl.BlockSpec((1, C), lambda i: (0, 0))
    return pl.pallas_call(
        _rwkv_post_body,
        grid=(m // tm,),
        in_specs=[blk] * 5 + [par] * 3,
        out_specs=blk,
        out_shape=jax.ShapeDtypeStruct((m, C), BF16),
        compiler_params=_cparams(("parallel",)),
        name="rwkv_post",
    )(y, r, k, v, g, row(r_k), row(ln_g), row(ln_b))


IN_PROJ_TN = 384


def _layout(segments, tn=IN_PROJ_TN):
    idx, valid, offs, pos = [], [], {}, 0
    for name, src, width in segments:
        offs[name] = pos
        src = np.asarray(src)
        idx.append(np.concatenate([src, np.zeros(width - len(src), np.int64)]))
        valid.append(np.concatenate([np.ones(len(src), bool), np.zeros(width - len(src), bool)]))
        pos += width
    total = -(-pos // tn) * tn
    idx.append(np.zeros(total - pos, np.int64))
    valid.append(np.zeros(total - pos, bool))
    return np.concatenate(idx), np.concatenate(valid), offs, total


def _even_layout():
    nq = NSA_HEADS * NSA_HEAD_DIM
    nkv = NSA_KV_GROUPS * NSA_HEAD_DIM
    inner = SSD_HEADS * SSD_HEAD_DIM
    conv_dim = inner + 2 * SSD_GROUPS * SSD_STATE
    o_gates = nq + 6 * nkv
    o_z = o_gates + 3 * NSA_HEADS
    o_xbc = o_z + inner
    o_dt = o_xbc + conv_dim
    ar = np.arange
    hpg = 3 * NSA_HEADS // NSA_KV_GROUPS
    dpg = SSD_HEADS // SSD_GROUPS
    segs = [("q", ar(0, nq), nq), ("kv", ar(nq, o_gates), 6 * nkv), ("z", ar(o_z, o_xbc), inner),
            ("xbc", ar(o_xbc, o_dt), conv_dim)]
    segs += [("gates" if g == 0 else f"gates{g}", o_gates + ar(g * hpg, (g + 1) * hpg), LANES)
             for g in range(NSA_KV_GROUPS)]
    segs += [("dt" if g == 0 else f"dt{g}", o_dt + ar(g * dpg, (g + 1) * dpg), LANES) for g in range(SSD_GROUPS)]
    return _layout(segs)


def _odd_layout():
    C = RWKV_HEADS * RWKV_HEAD_DIM
    o_kv = DSA_Q_RANK
    o_ki = o_kv + DSA_KV_RANK
    o_wi = o_ki + DSA_IDX_DIM
    o_rw = o_wi + DSA_IDX_HEADS
    n_rw = 3 * C + RWKV_W_LORA + RWKV_A_LORA + RWKV_G_LORA
    ar = np.arange
    segs = [("cq", ar(0, o_kv), DSA_Q_RANK), ("ckv", ar(o_kv, o_ki), DSA_KV_RANK), ("kidx", ar(o_ki, o_wi), LANES),
            ("widx", ar(o_wi, o_rw), LANES), ("rwkv", ar(o_rw, o_rw + n_rw), n_rw)]
    return _layout(segs)


def _permute_cols(w, idx, valid):
    return jnp.where(jnp.asarray(valid)[None, :], jnp.take(w, jnp.asarray(idx), axis=1), 0.0).astype(BF16)


def _even_mixer(h, g_norm, batch, w_in, w_out, rel_bias, pe_k, pe_v, w1_k, w2_k, w1_v, w2_v,
                conv_w, conv_b, dt_bias, a_log, d_skip, ssd_norm_g, g_out):
    m = h.shape[0]
    t = m // batch
    G, Dh = NSA_KV_GROUPS, NSA_HEAD_DIM
    idx, valid, cols, _ = _even_layout()
    p = norm_in_proj(h, g_norm, _permute_cols(w_in, idx, valid), tn=IN_PROJ_TN)
    kv = lax.slice_in_dim(p, cols["kv"], cols["kv"] + 6 * G * Dh, axis=1).reshape(batch, t, 6, G, Dh)
    rows16 = lambda u: (u.reshape(batch, t // NSA_CMP_STRIDE, NSA_CMP_STRIDE, G, Dh).transpose(0, 3, 1, 2, 4)
                        .reshape(batch, G, t // NSA_CMP_STRIDE, NSA_CMP_STRIDE * Dh))
    x_cmp = jnp.stack([rows16(kv[:, :, 0]), rows16(kv[:, :, 1])], axis=1)
    half = NSA_CMP_STRIDE * Dh
    pe = jnp.stack([pe_k.reshape(2, half), pe_v.reshape(2, half)]).astype(F32)
    w1 = jnp.stack([w1_k.reshape(2, half, -1), w1_v.reshape(2, half, -1)]).astype(BF16)
    w2 = jnp.stack([w2_k, w2_v]).astype(BF16)
    cmp_kv = nsa_compress(x_cmp, pe, w1, w2)
    tok = lambda i: kv[:, :, i].transpose(0, 2, 1, 3).astype(BF16)
    o_a = nsa_attention(p, cols, batch, cmp_kv, tok(2), tok(3), tok(4), tok(5), rel_bias)
    o_b = ssd_mixer(p, cols, batch, conv_w, conv_b, dt_bias, a_log, d_skip, ssd_norm_g)
    ka = o_a.shape[1]
    return out_proj_residual(o_a, o_b, w_out[:ka].astype(BF16), w_out[ka:].astype(BF16), g_out, h)


def _odd_mixer(h, g_norm, batch, w_in, w_out, rel_bias, q_norm_g, kv_norm_g, w_uq, w_qidx, w_uk, w_uv,
               mu, w0, w2, a0, a2, g2, k_k, k_a, r_k, ln_g, ln_b, g_out):
    m = h.shape[0]
    t = m // batch
    idx, valid, cols, _ = _odd_layout()
    p = norm_in_proj(h, g_norm, _permute_cols(w_in, idx, valid), tn=IN_PROJ_TN)
    H, Dh = DSA_HEADS, DSA_HEAD_DIM
    qa, qi, ckvn, ka, kb, widx = dsa_prep(
        p, cols, q_norm_g.astype(F32), kv_norm_g.astype(F32), w_uq.reshape(DSA_Q_RANK, H * Dh).astype(BF16),
        w_uk.transpose(1, 2, 0).astype(BF16), w_qidx.reshape(DSA_Q_RANK, -1).astype(BF16))
    o_c = dsa_attention(qa, qi, ckvn, ka, kb, widx, batch, rel_bias, w_uv.transpose(1, 0, 2).astype(BF16))
    r, lw, k, v, a, b, g = rwkv_prep(p, cols["rwkv"], t, mu, w0, w2, a0, a2, g2, k_k, k_a)
    y = rwkv_scan(r, lw, k, v, a, b, batch)
    o_d = rwkv_post(y, r, k, v, g, r_k, ln_g, ln_b)
    kc = o_c.shape[1]
    return out_proj_residual(o_c, o_d, w_out[:kc].astype(BF16), w_out[kc:].astype(BF16), g_out, h)


def kernel(x, norm_g, ffn_w_gate, ffn_w_up, ffn_w_down, rel_bias, even_w_in, even_w_out, nsa_pe_k, nsa_pe_v,
           nsa_cmp_w1_k, nsa_cmp_w2_k, nsa_cmp_w1_v, nsa_cmp_w2_v, ssd_conv_w, ssd_conv_b, ssd_dt_bias, ssd_a_log,
           ssd_d, ssd_norm_g, odd_w_in, odd_w_out, dsa_q_norm_g, dsa_kv_norm_g, dsa_w_uq, dsa_w_qidx, dsa_w_uk,
           dsa_w_uv, rwkv_mu, rwkv_w0, rwkv_w2, rwkv_a0, rwkv_a2, rwkv_g2, rwkv_k_k, rwkv_k_a, rwkv_r_k,
           rwkv_ln_g, rwkv_ln_b):
    batch, t, d = x.shape
    depth = norm_g.shape[0]
    h = x.reshape(batch * t, d).astype(F32)
    for layer in range(depth):
        g = norm_g[layer].astype(F32)
        i = layer // 2
        ffn = lambda hh, s, gi, go: ffn_half_step(hh, gi, ffn_w_gate[layer, s].astype(BF16),
                                                  ffn_w_up[layer, s].astype(BF16), ffn_w_down[layer, s].astype(BF16), go)
        h = ffn(h, 0, g[0], g[1])
        if layer % 2 == 0:
            h = _even_mixer(h, g[2], batch, even_w_in[i], even_w_out[i], rel_bias, nsa_pe_k[i], nsa_pe_v[i],
                            nsa_cmp_w1_k[i], nsa_cmp_w2_k[i], nsa_cmp_w1_v[i], nsa_cmp_w2_v[i], ssd_conv_w[i],
                            ssd_conv_b[i], ssd_dt_bias[i], ssd_a_log[i], ssd_d[i], ssd_norm_g[i], g[3])
        else:
            h = _odd_mixer(h, g[2], batch, odd_w_in[i], odd_w_out[i], rel_bias, dsa_q_norm_g[i], dsa_kv_norm_g[i],
                           dsa_w_uq[i], dsa_w_qidx[i], dsa_w_uk[i], dsa_w_uv[i], rwkv_mu[i], rwkv_w0[i], rwkv_w2[i],
                           rwkv_a0[i], rwkv_a2[i], rwkv_g2[i], rwkv_k_k[i], rwkv_k_a[i], rwkv_r_k[i],
                           rwkv_ln_g[i], rwkv_ln_b[i], g[3])
        h = ffn(h, 1, g[4], g[5])
    return h.reshape(batch, t, d).astype(x.dtype)
```

```python
import functools
import math

import jax
import jax.numpy as jnp
import numpy as np
from jax import lax
from jax.experimental import pallas as pl
from jax.experimental.pallas import tpu as pltpu

F32 = jnp.float32
BF16 = jnp.bfloat16

NORM_EPS = 1e-6
FFN_RES = 0.5
NEG_INF = -1e30
REL_BUCKETS = 32
REL_MAX_DIST = 128
Q_BLOCK = 128

NSA_HEADS = 8
NSA_HEAD_DIM = 128
NSA_KV_GROUPS = 2
NSA_CMP_LEN = 32
NSA_CMP_STRIDE = 16
NSA_SEL_LEN = 64
NSA_SEL_BLOCKS = 16
NSA_WINDOW = 512
NSA_FORCE = 1e4

SSD_HEAD_DIM = 64
SSD_HEADS = 32
SSD_GROUPS = 4
SSD_STATE = 128
SSD_CONV = 4
SSD_CHUNK = 256

DSA_HEADS = 8
DSA_HEAD_DIM = 128
DSA_Q_RANK = 512
DSA_KV_RANK = 256
DSA_IDX_HEADS = 16
DSA_IDX_DIM = 64
DSA_TOPK = 256

RWKV_HEAD_DIM = 64
RWKV_HEADS = 32
RWKV_W_LORA = 64
RWKV_A_LORA = 64
RWKV_G_LORA = 256
RWKV_LN_EPS = 64e-5
RWKV_CHUNK = 64

VMEM_LIMIT_V7X = 56 * 1024 * 1024
LANES = 128


def _cparams(sem):
    return pltpu.CompilerParams(dimension_semantics=sem, vmem_limit_bytes=VMEM_LIMIT_V7X)


def _rms(x, g, eps=NORM_EPS):
    return x * lax.rsqrt(jnp.mean(x * x, axis=-1, keepdims=True) + eps) * g


def _dot(a, b):
    return jnp.dot(a, b, preferred_element_type=F32)


def _dot_t(a, b):
    return lax.dot_general(a, b, (((1,), (1,)), ((), ())), preferred_element_type=F32)


def _dot_hi(a, b):
    return jnp.dot(a, b, preferred_element_type=F32, precision=lax.Precision.HIGHEST)


def _ffn_body(x_ref, gin_ref, wg_ref, wu_ref, wd_ref, gout_ref, o_ref, xn_ref, acc_ref):
    j = pl.program_id(1)

    @pl.when(j == 0)
    def _():
        xn_ref[...] = _rms(x_ref[...], gin_ref[...]).astype(BF16)
        acc_ref[...] = jnp.zeros_like(acc_ref)

    xn = xn_ref[...]
    g = _dot(xn, wg_ref[...])
    u = _dot(xn, wu_ref[...])
    h = (g * jax.nn.sigmoid(g)) * u
    acc_ref[...] += _dot(h.astype(BF16), wd_ref[...])

    @pl.when(j == pl.num_programs(1) - 1)
    def _():
        o_ref[...] = x_ref[...] + FFN_RES * _rms(acc_ref[...], gout_ref[...])


def ffn_half_step(h, g_in, wg, wu, wd, g_out, tm=512, tf=512):
    m, d = h.shape
    f = wg.shape[1]
    tm = min(tm, m)
    tf = min(tf, f)
    return pl.pallas_call(
        _ffn_body,
        grid=(m // tm, f // tf),
        in_specs=[
            pl.BlockSpec((tm, d), lambda i, j: (i, 0)),
            pl.BlockSpec((1, d), lambda i, j: (0, 0)),
            pl.BlockSpec((d, tf), lambda i, j: (0, j)),
            pl.BlockSpec((d, tf), lambda i, j: (0, j)),
            pl.BlockSpec((tf, d), lambda i, j: (j, 0)),
            pl.BlockSpec((1, d), lambda i, j: (0, 0)),
        ],
        out_specs=pl.BlockSpec((tm, d), lambda i, j: (i, 0)),
        out_shape=jax.ShapeDtypeStruct((m, d), F32),
        scratch_shapes=[pltpu.VMEM((tm, d), BF16), pltpu.VMEM((tm, d), F32)],
        compiler_params=_cparams(("parallel", "arbitrary")),
        name="ffn_half_step",
    )(h, g_in.reshape(1, d), wg, wu, wd, g_out.reshape(1, d))


def _in_proj_body(x_ref, g_ref, w_ref, o_ref, xn_ref):
    @pl.when(pl.program_id(1) == 0)
    def _():
        xn_ref[...] = _rms(x_ref[...], g_ref[...]).astype(BF16)

    o_ref[...] = _dot(xn_ref[...], w_ref[...])


def norm_in_proj(h, g, w, tm=1024, tn=512):
    m, d = h.shape
    n = w.shape[1]
    tm = min(tm, m)
    tn = min(tn, n)
    return pl.pallas_call(
        _in_proj_body,
        grid=(m // tm, n // tn),
        in_specs=[
            pl.BlockSpec((tm, d), lambda i, j: (i, 0)),
            pl.BlockSpec((1, d), lambda i, j: (0, 0)),
            pl.BlockSpec((d, tn), lambda i, j: (0, j)),
        ],
        out_specs=pl.BlockSpec((tm, tn), lambda i, j: (i, j)),
        out_shape=jax.ShapeDtypeStruct((m, n), F32),
        scratch_shapes=[pltpu.VMEM((tm, d), BF16)],
        compiler_params=_cparams(("parallel", "arbitrary")),
        name="norm_in_proj",
    )(h, g.reshape(1, d), w)


def _out_proj_body(a_ref, b_ref, wa_ref, wb_ref, g_ref, h_ref, o_ref):
    m = _dot(a_ref[...], wa_ref[...]) + _dot(b_ref[...], wb_ref[...])
    o_ref[...] = h_ref[...] + _rms(m, g_ref[...])


def out_proj_residual(oa, ob, wa, wb, g, h, tm=256):
    m, d = h.shape
    ka, kb = oa.shape[1], ob.shape[1]
    tm = min(tm, m)
    return pl.pallas_call(
        _out_proj_body,
        grid=(m // tm,),
        in_specs=[
            pl.BlockSpec((tm, ka), lambda i: (i, 0)),
            pl.BlockSpec((tm, kb), lambda i: (i, 0)),
            pl.BlockSpec((ka, d), lambda i: (0, 0)),
            pl.BlockSpec((kb, d), lambda i: (0, 0)),
            pl.BlockSpec((1, d), lambda i: (0, 0)),
            pl.BlockSpec((tm, d), lambda i: (i, 0)),
        ],
        out_specs=pl.BlockSpec((tm, d), lambda i: (i, 0)),
        out_shape=jax.ShapeDtypeStruct((m, d), F32),
        compiler_params=_cparams(("parallel",)),
        name="out_proj_residual",
    )(oa, ob, wa, wb, g.reshape(1, d), h)


def _rel_bucket(dist):
    n = jnp.maximum(dist, 0)
    exact = REL_BUCKETS // 2
    nf = jnp.maximum(n, 1).astype(F32)
    large = exact + (jnp.log(nf / exact) / math.log(REL_MAX_DIST / exact) * (REL_BUCKETS - exact)).astype(jnp.int32)
    large = jnp.minimum(large, REL_BUCKETS - 1)
    return jnp.where(n < exact, n, large)


def _bias_of_dist(rel_bias, dist):
    return jnp.moveaxis(rel_bias[_rel_bucket(dist)], -1, 0)


def _ssd_body(xs_ref, bm_ref, cm_ref, z_ref, dt_ref, cwx_ref, cwb_ref, cwc_ref, cbx_ref, cbb_ref, cbc_ref,
              dtb_ref, alog_ref, dskip_ref, ng_ref, o_ref, cat_ref, state_ref):
    c = pl.program_id(2)
    L = xs_ref.shape[0]
    HG = SSD_HEADS // SSD_GROUPS
    P = SSD_HEAD_DIM
    WX = HG * P

    @pl.when(c == 0)
    def _():
        cat_ref[0:8, :] = jnp.zeros((8, cat_ref.shape[1]), F32)
        state_ref[...] = jnp.zeros_like(state_ref)

    @pl.when(c > 0)
    def _():
        cat_ref[0:8, :] = cat_ref[L:L + 8, :]

    cat_ref[8:L + 8, 0:WX] = xs_ref[...]
    cat_ref[8:L + 8, WX:WX + SSD_STATE] = bm_ref[...]
    cat_ref[8:L + 8, WX + SSD_STATE:WX + 2 * SSD_STATE] = cm_ref[...]

    def conv(lo, hi, w_ref, b_ref):
        acc = b_ref[...] + w_ref[SSD_CONV - 1:SSD_CONV, :] * cat_ref[8:L + 8, lo:hi]
        for k in range(SSD_CONV - 1):
            s = SSD_CONV - 1 - k
            acc = acc + w_ref[k:k + 1, :] * cat_ref[pl.ds(8 - s, L), lo:hi]
        return acc * jax.nn.sigmoid(acc)

    xs = conv(0, WX, cwx_ref, cbx_ref)
    bm = conv(WX, WX + SSD_STATE, cwb_ref, cbb_ref)
    cm = conv(WX + SSD_STATE, WX + 2 * SSD_STATE, cwc_ref, cbc_ref)

    dt = jax.nn.softplus(dt_ref[...] + dtb_ref[...])
    a = -jnp.exp(alog_ref[...])
    row = lax.broadcasted_iota(jnp.int32, (L, L), 0)
    col = lax.broadcasted_iota(jnp.int32, (L, L), 1)
    causal = col <= row
    tril = causal.astype(F32)
    acum = _dot_hi(tril, dt * a)
    acum_t = acum.T
    er = lax.broadcasted_iota(jnp.int32, (LANES, WX), 0)
    ec = lax.broadcasted_iota(jnp.int32, (LANES, WX), 1)
    expand = (er == ec // P).astype(F32)
    dt_e = _dot_hi(dt, expand)
    acum_e = _dot_hi(acum, expand)
    last_e = acum_e[L - 1:L, :]
    xdt = xs * dt_e
    cm_b = cm.astype(BF16)
    bm_b = bm.astype(BF16)
    cb = _dot_t(cm_b, bm_b)
    state = state_ref[...]
    y_inter = _dot(cm_b, state.astype(BF16)) * jnp.exp(acum_e)
    xdt_b = xdt.astype(BF16)
    parts = []
    for h in range(HG):
        seg = acum[:, h:h + 1] - acum_t[h:h + 1, :]
        dec = jnp.exp(jnp.where(causal, seg, NEG_INF))
        gmat = (cb * dec).astype(BF16)
        parts.append(_dot(gmat, xdt_b[:, h * P:(h + 1) * P]))
    y = jnp.concatenate(parts, axis=1) + y_inter
    xw = (xdt * jnp.exp(last_e - acum_e)).astype(BF16)
    state_ref[...] = state * jnp.exp(last_e) + _dot(bm.T.astype(BF16), xw)
    y = y + dskip_ref[...] * xs
    zz = z_ref[...]
    y = y * (zz * jax.nn.sigmoid(zz))
    o_ref[...] = _rms(y, ng_ref[...]).astype(o_ref.dtype)


def ssd_mixer(p, cols, batch, conv_w, conv_b, dt_bias, a_log, d_skip, norm_g):
    m = p.shape[0]
    t = m // batch
    L = SSD_CHUNK
    nc = t // L
    G = SSD_GROUPS
    HG = SSD_HEADS // G
    WX = HG * SSD_HEAD_DIM
    N = SSD_STATE
    inner = SSD_HEADS * SSD_HEAD_DIM
    zb, xb, db = cols["z"] // WX, cols["xbc"] // WX, cols["dt"] // LANES
    bb = (cols["xbc"] + inner) // N
    cb_ = (cols["xbc"] + inner + G * N) // N

    def rows(b, g, c):
        return b * nc + c

    pad8 = lambda v: jnp.pad(v.reshape(G, 1, HG), ((0, 0), (0, 0), (0, LANES - HG)))
    cw = conv_w.astype(F32)
    cbias = conv_b.astype(F32).reshape(1, -1)
    cwx, cwb, cwc = cw[:, :inner], cw[:, inner:inner + G * N], cw[:, inner + G * N:]
    cbx, cbb, cbc = cbias[:, :inner], cbias[:, inner:inner + G * N], cbias[:, inner + G * N:]
    dsk = jnp.repeat(d_skip.astype(F32), SSD_HEAD_DIM).reshape(G, 1, WX)
    return pl.pallas_call(
        _ssd_body,
        grid=(batch, G, nc),
        in_specs=[
            pl.BlockSpec((L, WX), lambda b, g, c: (rows(b, g, c), xb + g)),
            pl.BlockSpec((L, N), lambda b, g, c: (rows(b, g, c), bb + g)),
            pl.BlockSpec((L, N), lambda b, g, c: (rows(b, g, c), cb_ + g)),
            pl.BlockSpec((L, WX), lambda b, g, c: (rows(b, g, c), zb + g)),
            pl.BlockSpec((L, LANES), lambda b, g, c: (rows(b, g, c), db + g)),
            pl.BlockSpec((SSD_CONV, WX), lambda b, g, c: (0, g)),
            pl.BlockSpec((SSD_CONV, N), lambda b, g, c: (0, g)),
            pl.BlockSpec((SSD_CONV, N), lambda b, g, c: (0, g)),
            pl.BlockSpec((1, WX), lambda b, g, c: (0, g)),
            pl.BlockSpec((1, N), lambda b, g, c: (0, g)),
            pl.BlockSpec((1, N), lambda b, g, c: (0, g)),
            pl.BlockSpec((None, 1, LANES), lambda b, g, c: (g, 0, 0)),
            pl.BlockSpec((None, 1, LANES), lambda b, g, c: (g, 0, 0)),
            pl.BlockSpec((None, 1, WX), lambda b, g, c: (g, 0, 0)),
            pl.BlockSpec((1, WX), lambda b, g, c: (0, g)),
        ],
        out_specs=pl.BlockSpec((L, WX), lambda b, g, c: (rows(b, g, c), g)),
        out_shape=jax.ShapeDtypeStruct((m, inner), BF16),
        scratch_shapes=[pltpu.VMEM((L + 8, WX + 2 * N), F32), pltpu.VMEM((N, WX), F32)],
        compiler_params=_cparams(("parallel", "parallel", "arbitrary")),
        name="ssd_mixer",
    )(p, p, p, p, p, cwx, cwb, cwc, cbx, cbb, cbc, pad8(dt_bias.astype(F32)), pad8(a_log.astype(F32)), dsk,
      norm_g.astype(F32).reshape(1, inner))


def _nsa_compress_body(x_ref, pe_ref, w1_ref, w2_ref, o_ref):
    x = x_ref[...]
    nr = x.shape[0]
    p0 = _dot((x + pe_ref[0:1, :]).astype(BF16), w1_ref[0])
    p1 = _dot((x + pe_ref[1:2, :]).astype(BF16), w1_ref[1])
    pre = p0 + pltpu.roll(p1, shift=nr - 1, axis=0)
    hdn = (pre * jax.nn.sigmoid(pre)).astype(BF16)
    out = _dot(hdn, w2_ref[...])
    valid = lax.broadcasted_iota(jnp.int32, out.shape, 0) < nr - 1
    o_ref[...] = jnp.where(valid, out, 0.0).astype(o_ref.dtype)


def nsa_compress(x, pe, w1, w2):
    b, _, g, nr, dk = x.shape
    hid = w1.shape[-1]
    dh = w2.shape[-1]
    return pl.pallas_call(
        _nsa_compress_body,
        grid=(b, 2, g),
        in_specs=[
            pl.BlockSpec((None, None, None, nr, dk), lambda i, s, j: (i, s, j, 0, 0)),
            pl.BlockSpec((None, 2, dk), lambda i, s, j: (s, 0, 0)),
            pl.BlockSpec((None, 2, dk, hid), lambda i, s, j: (s, 0, 0, 0)),
            pl.BlockSpec((None, hid, dh), lambda i, s, j: (s, 0, 0)),
        ],
        out_specs=pl.BlockSpec((None, None, None, nr, dh), lambda i, s, j: (i, s, j, 0, 0)),
        out_shape=jax.ShapeDtypeStruct((b, 2, g, nr, dh), BF16),
        compiler_params=_cparams(("parallel", "parallel", "parallel")),
        name="nsa_compress",
    )(x, pe, w1, w2)


NSA_CMP_PAD = 16
NSA_SEL_PAD = 128
NSA_FAR_TILE = 512
NSA_NEAR = 256


def _softmax_parts(parts):
    sm = [jnp.where(m, s, NEG_INF) for s, m in parts]
    mx = functools.reduce(jnp.maximum, [jnp.max(s, axis=-1, keepdims=True) for s in sm])
    es = [jnp.where(m, jnp.exp(s - mx), 0.0) for s, (_, m) in zip(sm, parts)]
    l = functools.reduce(jnp.add, [jnp.sum(e, axis=-1, keepdims=True) for e in es])
    inv = jnp.where(l > 0.0, 1.0 / l, 0.0)
    return [e * inv for e in es]


def _nsa_body(cfar_ref, q_ref, gate_ref, kc_ref, vc_ref, ks_ref, vs_ref, kw_ref, vw_ref,
              tc_ref, ts_ref, tw_ref, ov_ref, o_ref, st_ref, *, n_top):
    g = pl.program_id(1)
    qb = pl.program_id(2)
    Q = Q_BLOCK
    R = NSA_HEADS // NSA_KV_GROUPS
    Dh = NSA_HEAD_DIM
    t0 = qb * Q
    nck = kc_ref.shape[0] - 128

    def rep(x):
        return jnp.concatenate([x] * R, axis=0)

    q = q_ref[...] * (Dh ** -0.5)
    qs = jnp.concatenate([q[:, r * Dh:(r + 1) * Dh] for r in range(R)], axis=0).astype(BF16)
    cfar = jnp.concatenate([jnp.full((Q, 1), cfar_ref[g * R + r], F32) for r in range(R)], axis=0)
    i_q = lax.broadcasted_iota(jnp.int32, (Q, 1), 0)

    ncmp = nck - NSA_CMP_PAD
    kc_all = kc_ref[NSA_CMP_PAD:nck, :]
    vc_all = vc_ref[NSA_CMP_PAD:nck, :]
    n_far = lax.broadcasted_iota(jnp.int32, (1, ncmp), 1)
    mask_far = rep(jnp.broadcast_to(n_far < 8 * qb - NSA_CMP_PAD, (Q, ncmp)))
    s_far = _dot_t(qs, kc_all) + cfar
    near0 = pl.multiple_of(8 * qb, 8)
    kc_near = kc_ref[pl.ds(near0, 128), :]
    vc_near = vc_ref[pl.ds(near0, 128), :]
    m_near = lax.broadcasted_iota(jnp.int32, (1, 128), 1)
    dist_c = i_q - NSA_CMP_STRIDE * (m_near - NSA_CMP_PAD) - (NSA_CMP_LEN - 1)
    mask_near = rep((dist_c >= 0) & (m_near + 8 * qb - NSA_CMP_PAD >= 0) & (m_near < 2 * NSA_CMP_PAD))
    s_near = _dot_t(qs, kc_near) + tc_ref[...].reshape(R * Q, 128)
    p_far, p_near = _softmax_parts([(s_far, mask_far), (s_near, mask_near)])
    p_far = p_far.astype(BF16)
    p_near = p_near.astype(BF16)
    o_c = _dot(p_far, vc_all) + _dot(p_near, vc_near)
    imp4 = _dot(p_far, ov_ref[NSA_CMP_PAD:nck, :]) + _dot(p_near, ov_ref[pl.ds(near0, 128), :])
    imp = functools.reduce(jnp.add, [imp4[r * Q:(r + 1) * Q] for r in range(R)])

    j_blk = lax.broadcasted_iota(jnp.int32, (1, 128), 1)
    t_q = t0 + i_q
    cur = 2 * qb + (i_q >= NSA_SEL_LEN).astype(jnp.int32)
    forced = (j_blk == 0) | (j_blk == cur) | (j_blk == cur - 1)
    adm = j_blk * NSA_SEL_LEN <= t_q
    score = jnp.where(adm, imp + NSA_FORCE * forced.astype(F32), NEG_INF)
    st_ref[...] = score.T
    st = st_ref[...]
    j_row = lax.broadcasted_iota(jnp.int32, (128, Q), 0)

    def rank_body(jp, cnt):
        row = st_ref[pl.ds(jp, 1), :]
        ahead = (row > st) | ((row == st) & (jp < j_row))
        return cnt + ahead.astype(jnp.int32)

    cnt = lax.fori_loop(0, 128, rank_body, jnp.zeros((128, Q), jnp.int32), unroll=8)
    sel_b = (cnt < n_top).astype(F32).T.astype(BF16)

    jr = lax.broadcasted_iota(jnp.int32, (128, NSA_FAR_TILE), 0)
    jc = lax.broadcasted_iota(jnp.int32, (128, NSA_FAR_TILE), 1)
    blk_delta = jr - jc // NSA_SEL_LEN
    c_far = lax.broadcasted_iota(jnp.int32, (1, NSA_FAR_TILE), 1)

    def far_body(kt, carry):
        m_i, l_i, acc = carry
        k0 = pl.multiple_of(kt * NSA_FAR_TILE, NSA_FAR_TILE)
        k = ks_ref[pl.ds(NSA_SEL_PAD + k0, NSA_FAR_TILE), :]
        v = vs_ref[pl.ds(NSA_SEL_PAD + k0, NSA_FAR_TILE), :]
        expand = (blk_delta == kt * (NSA_FAR_TILE // NSA_SEL_LEN)).astype(BF16)
        member = _dot(sel_b, expand) > 0.5
        mask = rep(member & (k0 + c_far < t0 - NSA_SEL_PAD))
        s = jnp.where(mask, _dot_t(qs, k) + cfar, NEG_INF)
        m_new = jnp.maximum(m_i, jnp.max(s, axis=-1, keepdims=True))
        p = jnp.where(mask, jnp.exp(s - m_new), 0.0)
        alpha = jnp.exp(m_i - m_new)
        l_new = alpha * l_i + jnp.sum(p, axis=-1, keepdims=True)
        acc_new = alpha * acc + _dot(p.astype(BF16), v)
        return m_new, l_new, acc_new

    n_far_tiles = (qb + 2) // 4
    init = (jnp.full((R * Q, 1), NEG_INF, F32), jnp.zeros((R * Q, 1), F32), jnp.zeros((R * Q, Dh), F32))
    m_i, l_i, acc = lax.fori_loop(0, n_far_tiles, far_body, init)

    t0a = pl.multiple_of(t0, Q)
    k = ks_ref[pl.ds(t0a, NSA_NEAR), :]
    v = vs_ref[pl.ds(t0a, NSA_NEAR), :]
    c_near = lax.broadcasted_iota(jnp.int32, (1, NSA_NEAR), 1)
    pick = (blk_delta[:, :NSA_NEAR] == 2 * qb - NSA_SEL_PAD // NSA_SEL_LEN).astype(BF16)
    member = _dot(sel_b, pick) > 0.5
    mask = rep(member & (c_near <= i_q + NSA_SEL_PAD) & (c_near >= NSA_SEL_PAD - t0))
    s = jnp.where(mask, _dot_t(qs, k) + ts_ref[...].reshape(R * Q, NSA_NEAR), NEG_INF)
    m_new = jnp.maximum(m_i, jnp.max(s, axis=-1, keepdims=True))
    p = jnp.where(mask, jnp.exp(s - m_new), 0.0)
    alpha = jnp.exp(m_i - m_new)
    l_s = alpha * l_i + jnp.sum(p, axis=-1, keepdims=True)
    o_s = (alpha * acc + _dot(p.astype(BF16), v)) * jnp.where(l_s > 0.0, 1.0 / l_s, 0.0)

    wlen = NSA_WINDOW + Q
    kwin = kw_ref[pl.ds(t0a, wlen), :]
    vwin = vw_ref[pl.ds(t0a, wlen), :]
    c_w = lax.broadcasted_iota(jnp.int32, (1, wlen), 1)
    dist_w = i_q + NSA_WINDOW - c_w
    mask_w = rep((dist_w >= 0) & (dist_w < NSA_WINDOW) & (c_w >= NSA_WINDOW - t0))
    s_w = _dot_t(qs, kwin) + tw_ref[...].reshape(R * Q, wlen)
    (p_w,) = _softmax_parts([(s_w, mask_w)])
    o_w = _dot(p_w.astype(BF16), vwin)

    gates = jax.nn.sigmoid(gate_ref[...])
    outs = []
    for r in range(R):
        rs = slice(r * Q, (r + 1) * Q)
        outs.append(gates[:, 3 * r:3 * r + 1] * o_c[rs] + gates[:, 3 * r + 1:3 * r + 2] * o_s[rs]
                    + gates[:, 3 * r + 2:3 * r + 3] * o_w[rs])
    o_ref[...] = jnp.concatenate(outs, axis=1).astype(o_ref.dtype)


def nsa_attention(p, cols, batch, cmp_kv, sel_k, sel_v, win_k, win_v, rel_bias):
    m = p.shape[0]
    t = m // batch
    G = NSA_KV_GROUPS
    R = NSA_HEADS // G
    Dh = NSA_HEAD_DIM
    Q = Q_BLOCK
    nq = t // Q
    ncmp = t // NSA_CMP_STRIDE
    n_sel = t // NSA_SEL_LEN
    assert n_sel <= 128 and t % NSA_FAR_TILE == 0
    n_top = min(NSA_SEL_BLOCKS, n_sel)
    wlen = NSA_WINDOW + Q

    cmp_p = jnp.pad(cmp_kv, ((0, 0), (0, 0), (0, 0), (NSA_CMP_PAD, 128), (0, 0)))
    pad_t = lambda a, n: jnp.pad(a, ((0, 0), (0, 0), (n, 0), (0, 0)))
    ks_p, vs_p = pad_t(sel_k, NSA_SEL_PAD), pad_t(sel_v, NSA_SEL_PAD)
    kw_p, vw_p = pad_t(win_k, NSA_WINDOW), pad_t(win_v, NSA_WINDOW)

    i_q = jnp.arange(Q)[:, None]
    rb = rel_bias.astype(F32)
    tc = _bias_of_dist(rb, i_q - NSA_CMP_STRIDE * (jnp.arange(128)[None, :] - NSA_CMP_PAD) - (NSA_CMP_LEN - 1))
    ts = _bias_of_dist(rb, i_q + NSA_SEL_PAD - jnp.arange(NSA_NEAR)[None, :])
    tw = _bias_of_dist(rb, i_q + NSA_WINDOW - jnp.arange(wlen)[None, :])
    cfar = rb[REL_BUCKETS - 1]
    n_idx = np.arange(ncmp + NSA_CMP_PAD + 128)[:, None] - NSA_CMP_PAD
    j_idx = np.arange(128)[None, :]
    ov = ((n_idx >= 0) & (n_idx < ncmp - 1) & (n_idx * NSA_CMP_STRIDE < (j_idx + 1) * NSA_SEL_LEN)
          & (n_idx * NSA_CMP_STRIDE + NSA_CMP_LEN - 1 >= j_idx * NSA_SEL_LEN))
    ov = jnp.asarray(ov, BF16)

    qblk = cols["q"] // (R * Dh)
    gblk = cols["gates"] // LANES
    kv_spec = lambda rows_: pl.BlockSpec((None, None, rows_, Dh), lambda b, g, i: (b, g, 0, 0))
    return pl.pallas_call(
        functools.partial(_nsa_body, n_top=n_top),
        grid=(batch, G, nq),
        in_specs=[
            pl.BlockSpec(memory_space=pltpu.SMEM),
            pl.BlockSpec((Q, R * Dh), lambda b, g, i: (b * nq + i, qblk + g)),
            pl.BlockSpec((Q, LANES), lambda b, g, i: (b * nq + i, gblk + g)),
            pl.BlockSpec((None, None, None, ncmp + NSA_CMP_PAD + 128, Dh), lambda b, g, i: (b, 0, g, 0, 0)),
            pl.BlockSpec((None, None, None, ncmp + NSA_CMP_PAD + 128, Dh), lambda b, g, i: (b, 1, g, 0, 0)),
            kv_spec(t + NSA_SEL_PAD), kv_spec(t + NSA_SEL_PAD),
            kv_spec(t + NSA_WINDOW), kv_spec(t + NSA_WINDOW),
            pl.BlockSpec((R, Q, 128), lambda b, g, i: (g, 0, 0)),
            pl.BlockSpec((R, Q, NSA_NEAR), lambda b, g, i: (g, 0, 0)),
            pl.BlockSpec((R, Q, wlen), lambda b, g, i: (g, 0, 0)),
            pl.BlockSpec((ncmp + NSA_CMP_PAD + 128, 128), lambda b, g, i: (0, 0)),
        ],
        out_specs=pl.BlockSpec((Q, R * Dh), lambda b, g, i: (b * nq + i, g)),
        out_shape=jax.ShapeDtypeStruct((m, NSA_HEADS * Dh), BF16),
        scratch_shapes=[pltpu.VMEM((128, Q), F32)],
        compiler_params=_cparams(("parallel", "parallel", "arbitrary")),
        name="nsa_attention",
    )(cfar, p, p, cmp_p, cmp_p, ks_p, vs_p, kw_p, vw_p, tc, ts, tw, ov)


def _dsa_prep_body(cq_ref, ckv_ref, kidx_ref, widx_ref, gq_ref, gkv_ref, wuq_ref, wuk_ref, wqi_ref,
                   qa_ref, qi_ref, ckvn_ref, ka_ref, kb_ref, w_ref):
    H, Dh = DSA_HEADS, DSA_HEAD_DIM
    cq = _rms(cq_ref[...], gq_ref[...]).astype(BF16)
    ckvn_ref[...] = _rms(ckv_ref[...], gkv_ref[...]).astype(BF16)
    q = _dot(cq, wuq_ref[...]).astype(BF16)
    for h in range(H):
        qa = _dot(q[:, h * Dh:(h + 1) * Dh], wuk_ref[h]) * (Dh ** -0.5)
        qa_ref[:, h * DSA_KV_RANK:(h + 1) * DSA_KV_RANK] = qa.astype(BF16)
    qi_ref[...] = _dot(cq, wqi_ref[...]).astype(BF16)
    kidx = kidx_ref[...]
    lane = lax.broadcasted_iota(jnp.int32, kidx.shape, 1)
    ka_ref[...] = jnp.where(lane < DSA_IDX_DIM, kidx, 0.0).astype(BF16)
    kb_ref[...] = jnp.where(lane >= DSA_IDX_DIM, pltpu.roll(kidx, shift=DSA_IDX_DIM, axis=1), 0.0).astype(BF16)
    w_ref[...] = widx_ref[...] * (DSA_IDX_HEADS ** -0.5 * DSA_IDX_DIM ** -0.5)


def dsa_prep(p, cols, gq, gkv, wuq, wuk, wqi, tm=512):
    m = p.shape[0]
    tm = min(tm, m)
    H, Dh, C = DSA_HEADS, DSA_HEAD_DIM, DSA_KV_RANK
    full = lambda a: pl.BlockSpec(a.shape, lambda i: (0,) * a.ndim)
    gq2, gkv2 = gq.reshape(1, -1), gkv.reshape(1, -1)
    outs = [
        jax.ShapeDtypeStruct((m, H * C), BF16),
        jax.ShapeDtypeStruct((m, DSA_IDX_HEADS * DSA_IDX_DIM), BF16),
        jax.ShapeDtypeStruct((m, C), BF16),
        jax.ShapeDtypeStruct((m, LANES), BF16),
        jax.ShapeDtypeStruct((m, LANES), BF16),
        jax.ShapeDtypeStruct((m, LANES), F32),
    ]
    return pl.pallas_call(
        _dsa_prep_body,
        grid=(m // tm,),
        in_specs=[
            pl.BlockSpec((tm, DSA_Q_RANK), lambda i: (i, cols["cq"] // DSA_Q_RANK)),
            pl.BlockSpec((tm, C), lambda i: (i, cols["ckv"] // C)),
            pl.BlockSpec((tm, LANES), lambda i: (i, cols["kidx"] // LANES)),
            pl.BlockSpec((tm, LANES), lambda i: (i, cols["widx"] // LANES)),
            full(gq2), full(gkv2), full(wuq), full(wuk), full(wqi),
        ],
        out_specs=[pl.BlockSpec((tm, o.shape[1]), lambda i: (i, 0)) for o in outs],
        out_shape=outs,
        compiler_params=_cparams(("parallel",)),
        name="dsa_prep",
    )(p, p, p, p, gq2, gkv2, wuq, wuk, wqi)


DSA_TILE = 512
DSA_CHUNK = 128


def _key_to_f32(key):
    bits = key ^ ((key >> 31) & jnp.int32(0x7FFFFFFF))
    return lax.bitcast_convert_type(bits, F32)


def _dsa_body(cfar_ref, qi_ref, w_ref, qa_ref, ka_ref, kb_ref, ckv_ref, tb_ref, wuv_ref, o_ref,
              sc_ref, m_ref, l_ref, acc_ref, *, n_keep, idx_bits):
    qb = pl.program_id(1)
    Q = Q_BLOCK
    H = DSA_HEADS
    C = DSA_KV_RANK
    TK = DSA_TILE
    CPT = TK // DSA_CHUNK
    t0 = qb * Q
    i_q = lax.broadcasted_iota(jnp.int32, (Q, 1), 0)
    t_q = t0 + i_q
    n_tiles = qb // CPT + 1

    sc_ref[0] = jnp.full((Q, DSA_CHUNK), -jnp.inf, F32)
    qi = qi_ref[...]
    npair = DSA_IDX_HEADS // 2
    qstack = jnp.concatenate([qi[:, j * LANES:(j + 1) * LANES] for j in range(npair)], axis=0)
    w = w_ref[...]
    c_tile = lax.broadcasted_iota(jnp.int32, (1, TK), 1)

    def idx_body(kt, carry):
        k0 = pl.multiple_of(kt * TK, TK)
        se = _dot_t(qstack, ka_ref[pl.ds(DSA_CHUNK + k0, TK), :])
        so = _dot_t(qstack, kb_ref[pl.ds(DSA_CHUNK + k0, TK), :])
        tot = jnp.zeros((Q, TK), F32)
        for j in range(npair):
            rs = slice(j * Q, (j + 1) * Q)
            tot = tot + jnp.maximum(se[rs], 0.0) * w[:, 2 * j:2 * j + 1]
            tot = tot + jnp.maximum(so[rs], 0.0) * w[:, 2 * j + 1:2 * j + 2]
        tot = jnp.where(k0 + c_tile <= t_q, tot + 0.0, -jnp.inf)
        for u in range(CPT):
            sc_ref[1 + kt * CPT + u] = tot[:, u * DSA_CHUNK:(u + 1) * DSA_CHUNK]
        return carry

    lax.fori_loop(0, n_tiles, idx_body, 0)

    def tile_scores(kt):
        blk = sc_ref[pl.ds(1 + kt * CPT, CPT)]
        return jnp.concatenate([blk[u] for u in range(CPT)], axis=1)

    c_chunk = lax.broadcasted_iota(jnp.int32, (1, DSA_CHUNK), 1)

    def count(pred):
        def body(kt, acc):
            blk = sc_ref[pl.ds(1 + kt * CPT, CPT)]
            for u in range(CPT):
                acc = acc + pred(blk[u], kt * TK + u * DSA_CHUNK + c_chunk).astype(F32)
            return acc
        acc = lax.fori_loop(0, n_tiles, body, jnp.zeros((Q, DSA_CHUNK), F32))
        return jnp.sum(acc, axis=-1, keepdims=True)

    keep = jnp.float32(n_keep)

    def bis_body(it, key):
        cand = key + jnp.left_shift(jnp.int32(1), 31 - it)
        cf = _key_to_f32(cand)
        ok = count(lambda x, kt: x >= cf) >= keep
        return jnp.where(ok, cand, key)

    key = lax.fori_loop(0, 32, bis_body, jnp.full((Q, 1), jnp.iinfo(jnp.int32).min, jnp.int32))
    few = t_q < n_keep
    thr = jnp.where(few, -jnp.inf, _key_to_f32(key))
    n_gt = count(lambda x, kt: x > thr)
    n_ge = count(lambda x, kt: x >= thr)
    need = keep - n_gt
    tied = jnp.max(jnp.where(few, 0.0, n_ge - n_gt - need)) > 0.0

    def tie_cut(_):
        def body(it, qcut):
            cand = qcut + jnp.left_shift(jnp.int32(1), idx_bits - 1 - it)
            n = count(lambda x, pos: (x == thr) & (pos < cand))
            return jnp.where(n < need, cand, qcut)
        return lax.fori_loop(0, idx_bits, body, jnp.zeros((Q, 1), jnp.int32))

    big = jnp.full((Q, 1), jnp.iinfo(jnp.int32).max, jnp.int32)
    qcut = lax.cond(tied, tie_cut, lambda _: big, 0)
    qcut = jnp.where(few, big, qcut)

    def member_of(x, s_pos):
        return (x > thr) | ((x == thr) & (s_pos <= qcut))

    HG = 4
    NG = H // HG
    c_near = lax.broadcasted_iota(jnp.int32, (1, 2 * DSA_CHUNK), 1)
    rep = lambda x: jnp.concatenate([x] * HG, axis=0)
    qas = [jnp.concatenate([qa_ref[:, (hg * HG + r) * C:(hg * HG + r + 1) * C] for r in range(HG)], axis=0)
           for hg in range(NG)]
    cfars = [jnp.concatenate([jnp.full((Q, 1), cfar_ref[hg * HG + r], F32) for r in range(HG)], axis=0)
             for hg in range(NG)]
    m_ref[...] = jnp.full(m_ref.shape, NEG_INF, F32)
    l_ref[...] = jnp.zeros(l_ref.shape, F32)
    acc_ref[...] = jnp.zeros(acc_ref.shape, F32)

    def step(hg, s, mask, kv):
        rs = slice(hg * HG * Q, (hg + 1) * HG * Q)
        s = jnp.where(mask, s, NEG_INF)
        m_i = m_ref[rs, :]
        m_new = jnp.maximum(m_i, jnp.max(s, axis=-1, keepdims=True))
        p = jnp.where(mask, jnp.exp(s - m_new), 0.0)
        alpha = jnp.exp(m_i - m_new)
        l_ref[rs, :] = alpha * l_ref[rs, :] + jnp.sum(p, axis=-1, keepdims=True)
        acc_ref[rs, :] = alpha * acc_ref[rs, :] + _dot(p.astype(BF16), kv)
        m_ref[rs, :] = m_new

    def far_body(kt, carry):
        k0 = pl.multiple_of(kt * TK, TK)
        kv = ckv_ref[pl.ds(DSA_CHUNK + k0, TK), :]
        s_pos = k0 + c_tile
        mask = rep(member_of(tile_scores(kt), s_pos) & (s_pos < t0 - DSA_CHUNK))
        for hg in range(NG):
            step(hg, _dot_t(qas[hg], kv) + cfars[hg], mask, kv)
        return carry

    lax.fori_loop(0, (qb + 2) // CPT, far_body, 0)
    t0a = pl.multiple_of(t0, Q)
    kv = ckv_ref[pl.ds(t0a, 2 * DSA_CHUNK), :]
    blk = sc_ref[pl.ds(qb, 2)]
    x = jnp.concatenate([blk[0], blk[1]], axis=1)
    s_pos = t0 - DSA_CHUNK + c_near
    mask = rep(member_of(x, s_pos) & (s_pos <= t_q) & (s_pos >= 0))
    for hg in range(NG):
        step(hg, _dot_t(qas[hg], kv) + tb_ref[hg * HG:(hg + 1) * HG].reshape(HG * Q, 2 * DSA_CHUNK), mask, kv)
    l = l_ref[...]
    o = (acc_ref[...] * jnp.where(l > 0.0, 1.0 / l, 0.0)).astype(BF16)
    outs = [_dot(o[h * Q:(h + 1) * Q], wuv_ref[h]) for h in range(H)]
    o_ref[...] = jnp.concatenate(outs, axis=1).astype(o_ref.dtype)


def dsa_attention(qa, qi, ckvn, ka, kb, widx, batch, rel_bias, wuv):
    m = qa.shape[0]
    t = m // batch
    Q = Q_BLOCK
    nq = t // Q
    H, C, Dh = DSA_HEADS, DSA_KV_RANK, DSA_HEAD_DIM
    assert t % DSA_TILE == 0
    n_keep = min(DSA_TOPK, t // 4)
    idx_bits = int(t).bit_length()
    padk = lambda a: jnp.pad(a.reshape(batch, t, -1), ((0, 0), (DSA_CHUNK, 0), (0, 0)))
    ka_p, kb_p, ckv_p = padk(ka), padk(kb), padk(ckvn)
    rb = rel_bias.astype(F32)
    tb = _bias_of_dist(rb, jnp.arange(Q)[:, None] + DSA_CHUNK - jnp.arange(2 * DSA_CHUNK)[None, :])
    cfar = rb[REL_BUCKETS - 1]
    HG = 4
    return pl.pallas_call(
        functools.partial(_dsa_body, n_keep=n_keep, idx_bits=idx_bits),
        grid=(batch, nq),
        in_specs=[
            pl.BlockSpec(memory_space=pltpu.SMEM),
            pl.BlockSpec((Q, qi.shape[1]), lambda b, i: (b * nq + i, 0)),
            pl.BlockSpec((Q, LANES), lambda b, i: (b * nq + i, 0)),
            pl.BlockSpec((Q, H * C), lambda b, i: (b * nq + i, 0)),
            pl.BlockSpec((None, t + DSA_CHUNK, LANES), lambda b, i: (b, 0, 0)),
            pl.BlockSpec((None, t + DSA_CHUNK, LANES), lambda b, i: (b, 0, 0)),
            pl.BlockSpec((None, t + DSA_CHUNK, C), lambda b, i: (b, 0, 0)),
            pl.BlockSpec((H, Q, 2 * DSA_CHUNK), lambda b, i: (0, 0, 0)),
            pl.BlockSpec((H, C, Dh), lambda b, i: (0, 0, 0)),
        ],
        out_specs=pl.BlockSpec((Q, H * Dh), lambda b, i: (b * nq + i, 0)),
        out_shape=jax.ShapeDtypeStruct((m, H * Dh), BF16),
        scratch_shapes=[
            pltpu.VMEM((t // DSA_CHUNK + 1, Q, DSA_CHUNK), F32),
            pltpu.VMEM((H * Q, 1), F32),
            pltpu.VMEM((H * Q, 1), F32),
            pltpu.VMEM((H * Q, C), F32),
        ],
        compiler_params=_cparams(("parallel", "arbitrary")),
        name="dsa_attention",
    )(cfar, qi, widx, qa, ka_p, kb_p, ckv_p, tb, wuv)


def _head_sum(x, hd):
    r = lax.broadcasted_iota(jnp.int32, (LANES, LANES), 0) // hd
    c = lax.broadcasted_iota(jnp.int32, (LANES, LANES), 1) // hd
    seg = (r == c).astype(F32)
    n = x.shape[1] // LANES
    return jnp.concatenate([_dot_hi(x[:, j * LANES:(j + 1) * LANES], seg) for j in range(n)], axis=1)


def _rwkv_prep_body(x_ref, prev_ref, mu_ref, w0_ref, w2_ref, a0_ref, a2_ref, g2_ref, kk_ref, ka_ref,
                    r_ref, g_cum_ref, g_exc_ref, k_ref, v_ref, a_ref, b_ref, g_ref, *, rows_per_seq):
    i = pl.program_id(0)
    C = RWKV_HEADS * RWKV_HEAD_DIM
    x = x_ref[...]
    tm = x.shape[0]
    first = (i * tm) % rows_per_seq == 0
    prev_row = jnp.where(first, 0.0, prev_ref[7:8, :])
    rolled = pltpu.roll(x, shift=1, axis=0)
    prev = jnp.where(lax.broadcasted_iota(jnp.int32, x.shape, 0) == 0, prev_row, rolled)
    x = x + (prev - x) * mu_ref[...]
    r, k, v = x[:, 0:C], x[:, C:2 * C], x[:, 2 * C:3 * C]
    wl = x[:, 3 * C:3 * C + RWKV_W_LORA]
    al = x[:, 3 * C + RWKV_W_LORA:3 * C + LANES]
    gl = x[:, 3 * C + LANES:3 * C + LANES + RWKV_G_LORA]
    w = -jax.nn.softplus(-(w0_ref[...] + _dot(jnp.tanh(wl).astype(BF16), w2_ref[...]))) - 0.5
    eta = jax.nn.sigmoid(a0_ref[...] + _dot(al.astype(BF16), a2_ref[...]))
    g_ref[...] = _dot(jax.nn.sigmoid(gl).astype(BF16), g2_ref[...])
    kk = k * kk_ref[...]
    kk = kk / jnp.maximum(jnp.sqrt(_head_sum(kk * kk, RWKV_HEAD_DIM)), 1e-12)
    r_ref[...] = r
    lw = -jnp.exp(w)
    ti = lax.broadcasted_iota(jnp.int32, (tm, tm), 0)
    si = lax.broadcasted_iota(jnp.int32, (tm, tm), 1)
    in_chunk = ((si <= ti) & (ti // RWKV_CHUNK == si // RWKV_CHUNK)).astype(F32)
    gcum = _dot_hi(in_chunk, lw)
    g_cum_ref[...] = gcum
    g_exc_ref[...] = gcum - lw
    k_ref[...] = k * (1.0 + (eta - 1.0) * ka_ref[...])
    v_ref[...] = v
    a_ref[...] = -kk
    b_ref[...] = kk * eta


def rwkv_prep(p, col, seq_len, mu, w0, w2, a0, a2, g2, k_k, k_a, tm=256):
    m = p.shape[0]
    C = RWKV_HEADS * RWKV_HEAD_DIM
    width = 3 * C + LANES + RWKV_G_LORA
    tm = min(tm, m)
    assert col % LANES == 0 and seq_len % tm == 0 and tm % RWKV_CHUNK == 0
    x = lax.slice_in_dim(p, col, col + width, axis=1)
    row = lambda a: a.astype(F32).reshape(1, -1)
    full = lambda a: pl.BlockSpec(a.shape, lambda i: (0,) * a.ndim)
    args = [row(mu), row(w0), w2.astype(BF16), row(a0), a2.astype(BF16), g2.astype(BF16), row(k_k), row(k_a)]
    out = jax.ShapeDtypeStruct((m, C), F32)
    return pl.pallas_call(
        functools.partial(_rwkv_prep_body, rows_per_seq=seq_len),
        grid=(m // tm,),
        in_specs=[pl.BlockSpec((tm, width), lambda i: (i, 0)),
                  pl.BlockSpec((8, width), lambda i: (jnp.maximum(i * (tm // 8) - 1, 0), 0))]
                 + [full(a) for a in args],
        out_specs=[pl.BlockSpec((tm, C), lambda i: (i, 0))] * 8,
        out_shape=[out] * 8,
        compiler_params=_cparams(("parallel",)),
        name="rwkv_prep",
    )(x, x, *args)


RWKV_HB = 4


RWKV_GROUPS_PER_STEP = 2


def _rwkv_scan_body(r_ref, g_ref, ge_ref, k_ref, v_ref, a_ref, b_ref, gt_ref, kt_ref, bt_ref, y_ref, st_ref):
    c = pl.program_id(1)
    nb, L, _ = r_ref.shape
    N = RWKV_HEAD_DIM
    HB = RWKV_HB
    W = HB * N
    assert L == N

    @pl.when(c == 0)
    def _():
        st_ref[...] = jnp.zeros_like(st_ref)

    ri = lax.broadcasted_iota(jnp.int32, (W, W), 0)
    ci = lax.broadcasted_iota(jnp.int32, (W, W), 1)
    own = (ri // N) == (ci // N)
    strict = (ci % L) < (ri % L)
    lower = (ci % L) <= (ri % L)
    eye = (ri == ci).astype(F32)
    n_sq = max(int(L - 1).bit_length() - 1, 0)

    def stack(x):
        return jnp.where(own, jnp.concatenate([x] * HB, axis=0), 0.0).astype(BF16)

    def rows_of_heads(x):
        return jnp.concatenate([x[:, h * N:(h + 1) * N] for h in range(HB)], axis=0)

    for bi in range(nb):
        ys = []
        for gi in range(RWKV_GROUPS_PER_STEP):
            ls = slice(gi * W, (gi + 1) * W)
            g = g_ref[bi, :, ls]
            ieg = jnp.exp(-g)
            at = stack(a_ref[bi, :, ls] * jnp.exp(ge_ref[bi, :, ls]))
            rt = stack(r_ref[bi, :, ls] * jnp.exp(g))
            bt = stack(b_ref[bi, :, ls] * ieg)
            kt = stack(k_ref[bi, :, ls] * ieg)
            v_b = rows_of_heads(v_ref[bi, :, ls]).astype(BF16)
            gt = gt_ref[bi, ls, :]
            wend = jnp.exp(gt[:, L - 1:L] - gt)
            bh = jnp.where(own, jnp.concatenate([bt_ref[bi, ls, :] * wend] * HB, axis=1), 0.0).astype(BF16)
            kh = jnp.where(own, jnp.concatenate([kt_ref[bi, ls, :] * wend] * HB, axis=1), 0.0).astype(BF16)
            aa = _dot_t(jnp.concatenate([at, rt], axis=0), jnp.concatenate([bt, kt], axis=0))
            a_ab = jnp.where(strict, aa[0:W, 0:W], 0.0)
            a_ak = jnp.where(strict, aa[0:W, W:2 * W], 0.0).astype(BF16)
            m_rb = jnp.where(lower, aa[W:2 * W, 0:W], 0.0).astype(BF16)
            m_rk = jnp.where(lower, aa[W:2 * W, W:2 * W], 0.0).astype(BF16)
            tinv = eye + a_ab
            pw = a_ab
            for _ in range(n_sq):
                pw_b = pw.astype(BF16)
                pw = _dot(pw_b, pw_b)
                tinv = tinv + _dot(tinv.astype(BF16), pw.astype(BF16))
            st = st_ref[bi, gi]
            st_b = st.astype(BF16)
            x = _dot(at, st_b) + _dot(a_ak, v_b)
            u_b = _dot(tinv.astype(BF16), x.astype(BF16)).astype(BF16)
            y = _dot(rt, st_b) + _dot(m_rb, u_b) + _dot(m_rk, v_b)
            st_ref[bi, gi] = jnp.exp(gt[:, L - 1:L]) * st + _dot(bh, u_b) + _dot(kh, v_b)
            ys += [y[h * L:(h + 1) * L] for h in range(HB)]
        y_ref[bi] = jnp.concatenate(ys, axis=1)


def rwkv_scan(r, g, ge, k, v, a, b, batch):
    m, C = r.shape
    t = m // batch
    L = RWKV_CHUNK
    nc = t // L
    W = RWKV_GROUPS_PER_STEP * RWKV_HB * RWKV_HEAD_DIM
    tmaj = pl.BlockSpec((batch, L, W), lambda h, c: (0, c, h))
    cmaj = pl.BlockSpec((batch, None, W, L), lambda h, c: (0, c, h, 0))
    seq = lambda z: z.reshape(batch, t, C)
    chan = lambda z: z.reshape(batch, nc, L, C).transpose(0, 1, 3, 2)
    y = pl.pallas_call(
        _rwkv_scan_body,
        grid=(C // W, nc),
        in_specs=[tmaj] * 7 + [cmaj] * 3,
        out_specs=tmaj,
        out_shape=jax.ShapeDtypeStruct((batch, t, C), F32),
        scratch_shapes=[pltpu.VMEM((batch, RWKV_GROUPS_PER_STEP, RWKV_HB * RWKV_HEAD_DIM, RWKV_HEAD_DIM), F32)],
        compiler_params=_cparams(("parallel", "arbitrary")),
        name="rwkv_scan",
    )(seq(r), seq(g), seq(ge), seq(k), seq(v), seq(a), seq(b), chan(g), chan(k), chan(b))
    return y.reshape(m, C)


def _rwkv_post_body(y_ref, r_ref, k_ref, v_ref, g_ref, rk_ref, lng_ref, lnb_ref, o_ref):
    N = RWKV_HEAD_DIM
    y = y_ref[...]
    mean = _head_sum(y, N) * (1.0 / N)
    yc = y - mean
    var = _head_sum(yc * yc, N) * (1.0 / N)
    yn = yc * lax.rsqrt(var + RWKV_LN_EPS) * lng_ref[...] + lnb_ref[...]
    bonus = _head_sum(r_ref[...] * k_ref[...] * rk_ref[...], N) * v_ref[...]
    o_ref[...] = ((yn + bonus) * g_ref[...]).astype(o_ref.dtype)


def rwkv_post(y, r, k, v, g, r_k, ln_g, ln_b, tm=256):
    m, C = y.shape
    tm = min(tm, m)
    row = lambda a: a.astype(F32).reshape(1, C)
    blk = pl.BlockSpec((tm, C), lambda i: (i, 0))
    par = pl.BlockSpec((1, C), lambda i: (0, 0))
    return pl.pallas_call(
        _rwkv_post_body,
        grid=(m // tm,),
        in_specs=[blk] * 5 + [par] * 3,
        out_specs=blk,
        out_shape=jax.ShapeDtypeStruct((m, C), BF16),
        compiler_params=_cparams(("parallel",)),
        name="rwkv_post",
    )(y, r, k, v, g, row(r_k), row(ln_g), row(ln_b))


IN_PROJ_TN = 384


def _layout(segments, tn=IN_PROJ_TN):
    idx, valid, offs, pos = [], [], {}, 0
    for name, src, width in segments:
        offs[name] = pos
        src = np.asarray(src)
        idx.append(np.concatenate([src, np.zeros(width - len(src), np.int64)]))
        valid.append(np.concatenate([np.ones(len(src), bool), np.zeros(width - len(src), bool)]))
        pos += width
    total = -(-pos // tn) * tn
    idx.append(np.zeros(total - pos, np.int64))
    valid.append(np.zeros(total - pos, bool))
    return np.concatenate(idx), np.concatenate(valid), offs, total


def _even_layout():
    nq = NSA_HEADS * NSA_HEAD_DIM
    nkv = NSA_KV_GROUPS * NSA_HEAD_DIM
    inner = SSD_HEADS * SSD_HEAD_DIM
    conv_dim = inner + 2 * SSD_GROUPS * SSD_STATE
    o_gates = nq + 6 * nkv
    o_z = o_gates + 3 * NSA_HEADS
    o_xbc = o_z + inner
    o_dt = o_xbc + conv_dim
    ar = np.arange
    hpg = 3 * NSA_HEADS // NSA_KV_GROUPS
    dpg = SSD_HEADS // SSD_GROUPS
    segs = [("q", ar(0, nq), nq), ("kv", ar(nq, o_gates), 6 * nkv), ("z", ar(o_z, o_xbc), inner),
            ("xbc", ar(o_xbc, o_dt), conv_dim)]
    segs += [("gates" if g == 0 else f"gates{g}", o_gates + ar(g * hpg, (g + 1) * hpg), LANES)
             for g in range(NSA_KV_GROUPS)]
    segs += [("dt" if g == 0 else f"dt{g}", o_dt + ar(g * dpg, (g + 1) * dpg), LANES) for g in range(SSD_GROUPS)]
    return _layout(segs)


def _odd_layout():
    C = RWKV_HEADS * RWKV_HEAD_DIM
    o_kv = DSA_Q_RANK
    o_ki = o_kv + DSA_KV_RANK
    o_wi = o_ki + DSA_IDX_DIM
    o_rw = o_wi + DSA_IDX_HEADS
    n_rw = 3 * C + RWKV_W_LORA + RWKV_A_LORA + RWKV_G_LORA
    ar = np.arange
    segs = [("cq", ar(0, o_kv), DSA_Q_RANK), ("ckv", ar(o_kv, o_ki), DSA_KV_RANK), ("kidx", ar(o_ki, o_wi), LANES),
            ("widx", ar(o_wi, o_rw), LANES), ("rwkv", ar(o_rw, o_rw + n_rw), n_rw)]
    return _layout(segs)


def _permute_cols(w, idx, valid):
    return jnp.where(jnp.asarray(valid)[None, :], jnp.take(w, jnp.asarray(idx), axis=1), 0.0).astype(BF16)


def _even_mixer(h, g_norm, batch, w_in, w_out, rel_bias, pe_k, pe_v, w1_k, w2_k, w1_v, w2_v,
                conv_w, conv_b, dt_bias, a_log, d_skip, ssd_norm_g, g_out):
    m = h.shape[0]
    t = m // batch
    G, Dh = NSA_KV_GROUPS, NSA_HEAD_DIM
    idx, valid, cols, _ = _even_layout()
    p = norm_in_proj(h, g_norm, _permute_cols(w_in, idx, valid), tn=IN_PROJ_TN)
    kv = lax.slice_in_dim(p, cols["kv"], cols["kv"] + 6 * G * Dh, axis=1).reshape(batch, t, 6, G, Dh)
    rows16 = lambda u: (u.reshape(batch, t // NSA_CMP_STRIDE, NSA_CMP_STRIDE, G, Dh).transpose(0, 3, 1, 2, 4)
                        .reshape(batch, G, t // NSA_CMP_STRIDE, NSA_CMP_STRIDE * Dh))
    x_cmp = jnp.stack([rows16(kv[:, :, 0]), rows16(kv[:, :, 1])], axis=1)
    half = NSA_CMP_STRIDE * Dh
    pe = jnp.stack([pe_k.reshape(2, half), pe_v.reshape(2, half)]).astype(F32)
    w1 = jnp.stack([w1_k.reshape(2, half, -1), w1_v.reshape(2, half, -1)]).astype(BF16)
    w2 = jnp.stack([w2_k, w2_v]).astype(BF16)
    cmp_kv = nsa_compress(x_cmp, pe, w1, w2)
    tok = lambda i: kv[:, :, i].transpose(0, 2, 1, 3).astype(BF16)
    o_a = nsa_attention(p, cols, batch, cmp_kv, tok(2), tok(3), tok(4), tok(5), rel_bias)
    o_b = ssd_mixer(p, cols, batch, conv_w, conv_b, dt_bias, a_log, d_skip, ssd_norm_g)
    ka = o_a.shape[1]
    return out_proj_residual(o_a, o_b, w_out[:ka].astype(BF16), w_out[ka:].astype(BF16), g_out, h)


def _odd_mixer(h, g_norm, batch, w_in, w_out, rel_bias, q_norm_g, kv_norm_g, w_uq, w_qidx, w_uk, w_uv,
               mu, w0, w2, a0, a2, g2, k_k, k_a, r_k, ln_g, ln_b, g_out):
    m = h.shape[0]
    t = m // batch
    idx, valid, cols, _ = _odd_layout()
    p = norm_in_proj(h, g_norm, _permute_cols(w_in, idx, valid), tn=IN_PROJ_TN)
    H, Dh = DSA_HEADS, DSA_HEAD_DIM
    qa, qi, ckvn, ka, kb, widx = dsa_prep(
        p, cols, q_norm_g.astype(F32), kv_norm_g.astype(F32), w_uq.reshape(DSA_Q_RANK, H * Dh).astype(BF16),
        w_uk.transpose(1, 2, 0).astype(BF16), w_qidx.reshape(DSA_Q_RANK, -1).astype(BF16))
    o_c = dsa_attention(qa, qi, ckvn, ka, kb, widx, batch, rel_bias, w_uv.transpose(1, 0, 2).astype(BF16))
    r, gc, ge, k, v, a, b, g = rwkv_prep(p, cols["rwkv"], t, mu, w0, w2, a0, a2, g2, k_k, k_a)
    y = rwkv_scan(r, gc, ge, k, v, a, b, batch)
    o_d = rwkv_post(y, r, k, v, g, r_k, ln_g, ln_b)
    kc = o_c.shape[1]
    return out_proj_residual(o_c, o_d, w_out[:kc].astype(BF16), w_out[kc:].astype(BF16), g_out, h)


def kernel(x, norm_g, ffn_w_gate, ffn_w_up, ffn_w_down, rel_bias, even_w_in, even_w_out, nsa_pe_k, nsa_pe_v,
           nsa_cmp_w1_k, nsa_cmp_w2_k, nsa_cmp_w1_v, nsa_cmp_w2_v, ssd_conv_w, ssd_conv_b, ssd_dt_bias, ssd_a_log,
           ssd_d, ssd_norm_g, odd_w_in, odd_w_out, dsa_q_norm_g, dsa_kv_norm_g, dsa_w_uq, dsa_w_qidx, dsa_w_uk,
           dsa_w_uv, rwkv_mu, rwkv_w0, rwkv_w2, rwkv_a0, rwkv_a2, rwkv_g2, rwkv_k_k, rwkv_k_a, rwkv_r_k,
           rwkv_ln_g, rwkv_ln_b):
    batch, t, d = x.shape
    depth = norm_g.shape[0]
    h = x.reshape(batch * t, d).astype(F32)
    for layer in range(depth):
        g = norm_g[layer].astype(F32)
        i = layer // 2
        ffn = lambda hh, s, gi, go: ffn_half_step(hh, gi, ffn_w_gate[layer, s].astype(BF16),
                                                  ffn_w_up[layer, s].astype(BF16), ffn_w_down[layer, s].astype(BF16), go)
        h = ffn(h, 0, g[0], g[1])
        if layer % 2 == 0:
            h = _even_mixer(h, g[2], batch, even_w_in[i], even_w_out[i], rel_bias, nsa_pe_k[i], nsa_pe_v[i],
                            nsa_cmp_w1_k[i], nsa_cmp_w2_k[i], nsa_cmp_w1_v[i], nsa_cmp_w2_v[i], ssd_conv_w[i],
                            ssd_conv_b[i], ssd_dt_bias[i], ssd_a_log[i], ssd_d[i], ssd_norm_g[i], g[3])
        else:
            h = _odd_mixer(h, g[2], batch, odd_w_in[i], odd_w_out[i], rel_bias, dsa_q_norm_g[i], dsa_kv_norm_g[i],
                           dsa_w_uq[i], dsa_w_qidx[i], dsa_w_uk[i], dsa_w_uv[i], rwkv_mu[i], rwkv_w0[i], rwkv_w2[i],
                           rwkv_a0[i], rwkv_a2[i], rwkv_g2[i], rwkv_k_k[i], rwkv_k_a[i], rwkv_r_k[i],
                           rwkv_ln_g[i], rwkv_ln_b[i], g[3])
        h = ffn(h, 1, g[4], g[5])
    return h.reshape(batch, t, d).astype(x.dtype)
```

```python
import functools
import math

import jax
import jax.numpy as jnp
import numpy as np
from jax import lax
from jax.experimental import pallas as pl
from jax.experimental.pallas import tpu as pltpu

F32 = jnp.float32
BF16 = jnp.bfloat16

NORM_EPS = 1e-6
FFN_RES = 0.5
NEG_INF = -1e30
REL_BUCKETS = 32
REL_MAX_DIST = 128
Q_BLOCK = 128

NSA_HEADS = 8
NSA_HEAD_DIM = 128
NSA_KV_GROUPS = 2
NSA_CMP_LEN = 32
NSA_CMP_STRIDE = 16
NSA_SEL_LEN = 64
NSA_SEL_BLOCKS = 16
NSA_WINDOW = 512
NSA_FORCE = 1e4

SSD_HEAD_DIM = 64
SSD_HEADS = 32
SSD_GROUPS = 4
SSD_STATE = 128
SSD_CONV = 4
SSD_CHUNK = 256

DSA_HEADS = 8
DSA_HEAD_DIM = 128
DSA_Q_RANK = 512
DSA_KV_RANK = 256
DSA_IDX_HEADS = 16
DSA_IDX_DIM = 64
DSA_TOPK = 256

RWKV_HEAD_DIM = 64
RWKV_HEADS = 32
RWKV_W_LORA = 64
RWKV_A_LORA = 64
RWKV_G_LORA = 256
RWKV_LN_EPS = 64e-5
RWKV_CHUNK = 64

VMEM_LIMIT_V7X = 56 * 1024 * 1024
LANES = 128


def _cparams(sem):
    return pltpu.CompilerParams(dimension_semantics=sem, vmem_limit_bytes=VMEM_LIMIT_V7X)


def _rms(x, g, eps=NORM_EPS):
    return x * lax.rsqrt(jnp.mean(x * x, axis=-1, keepdims=True) + eps) * g


def _dot(a, b):
    return jnp.dot(a, b, preferred_element_type=F32)


def _dot_t(a, b):
    return lax.dot_general(a, b, (((1,), (1,)), ((), ())), preferred_element_type=F32)


def _dot_hi(a, b):
    return jnp.dot(a, b, preferred_element_type=F32, precision=lax.Precision.HIGHEST)


def _ffn_body(x_ref, gin_ref, wg_ref, wu_ref, wd_ref, gout_ref, o_ref, xn_ref, acc_ref):
    j = pl.program_id(1)

    @pl.when(j == 0)
    def _():
        xn_ref[...] = _rms(x_ref[...], gin_ref[...]).astype(BF16)
        acc_ref[...] = jnp.zeros_like(acc_ref)

    xn = xn_ref[...]
    g = _dot(xn, wg_ref[...])
    u = _dot(xn, wu_ref[...])
    h = (g * jax.nn.sigmoid(g)) * u
    acc_ref[...] += _dot(h.astype(BF16), wd_ref[...])

    @pl.when(j == pl.num_programs(1) - 1)
    def _():
        o_ref[...] = x_ref[...] + FFN_RES * _rms(acc_ref[...], gout_ref[...])


def ffn_half_step(h, g_in, wg, wu, wd, g_out, tm=512, tf=512):
    m, d = h.shape
    f = wg.shape[1]
    tm = min(tm, m)
    tf = min(tf, f)
    return pl.pallas_call(
        _ffn_body,
        grid=(m // tm, f // tf),
        in_specs=[
            pl.BlockSpec((tm, d), lambda i, j: (i, 0)),
            pl.BlockSpec((1, d), lambda i, j: (0, 0)),
            pl.BlockSpec((d, tf), lambda i, j: (0, j)),
            pl.BlockSpec((d, tf), lambda i, j: (0, j)),
            pl.BlockSpec((tf, d), lambda i, j: (j, 0)),
            pl.BlockSpec((1, d), lambda i, j: (0, 0)),
        ],
        out_specs=pl.BlockSpec((tm, d), lambda i, j: (i, 0)),
        out_shape=jax.ShapeDtypeStruct((m, d), F32),
        scratch_shapes=[pltpu.VMEM((tm, d), BF16), pltpu.VMEM((tm, d), F32)],
        compiler_params=_cparams(("parallel", "arbitrary")),
        name="ffn_half_step",
    )(h, g_in.reshape(1, d), wg, wu, wd, g_out.reshape(1, d))


def _in_proj_body(x_ref, g_ref, w_ref, o_ref, xn_ref):
    @pl.when(pl.program_id(1) == 0)
    def _():
        xn_ref[...] = _rms(x_ref[...], g_ref[...]).astype(BF16)

    o_ref[...] = _dot(xn_ref[...], w_ref[...])


def norm_in_proj(h, g, w, tm=1024, tn=512):
    m, d = h.shape
    n = w.shape[1]
    tm = min(tm, m)
    tn = min(tn, n)
    return pl.pallas_call(
        _in_proj_body,
        grid=(m // tm, n // tn),
        in_specs=[
            pl.BlockSpec((tm, d), lambda i, j: (i, 0)),
            pl.BlockSpec((1, d), lambda i, j: (0, 0)),
            pl.BlockSpec((d, tn), lambda i, j: (0, j)),
        ],
        out_specs=pl.BlockSpec((tm, tn), lambda i, j: (i, j)),
        out_shape=jax.ShapeDtypeStruct((m, n), F32),
        scratch_shapes=[pltpu.VMEM((tm, d), BF16)],
        compiler_params=_cparams(("parallel", "arbitrary")),
        name="norm_in_proj",
    )(h, g.reshape(1, d), w)


def _out_proj_body(a_ref, b_ref, wa_ref, wb_ref, g_ref, h_ref, o_ref):
    m = _dot(a_ref[...], wa_ref[...]) + _dot(b_ref[...], wb_ref[...])
    o_ref[...] = h_ref[...] + _rms(m, g_ref[...])


def out_proj_residual(oa, ob, wa, wb, g, h, tm=256):
    m, d = h.shape
    ka, kb = oa.shape[1], ob.shape[1]
    tm = min(tm, m)
    return pl.pallas_call(
        _out_proj_body,
        grid=(m // tm,),
        in_specs=[
            pl.BlockSpec((tm, ka), lambda i: (i, 0)),
            pl.BlockSpec((tm, kb), lambda i: (i, 0)),
            pl.BlockSpec((ka, d), lambda i: (0, 0)),
            pl.BlockSpec((kb, d), lambda i: (0, 0)),
            pl.BlockSpec((1, d), lambda i: (0, 0)),
            pl.BlockSpec((tm, d), lambda i: (i, 0)),
        ],
        out_specs=pl.BlockSpec((tm, d), lambda i: (i, 0)),
        out_shape=jax.ShapeDtypeStruct((m, d), F32),
        compiler_params=_cparams(("parallel",)),
        name="out_proj_residual",
    )(oa, ob, wa, wb, g.reshape(1, d), h)


def _rel_bucket(dist):
    n = jnp.maximum(dist, 0)
    exact = REL_BUCKETS // 2
    nf = jnp.maximum(n, 1).astype(F32)
    large = exact + (jnp.log(nf / exact) / math.log(REL_MAX_DIST / exact) * (REL_BUCKETS - exact)).astype(jnp.int32)
    large = jnp.minimum(large, REL_BUCKETS - 1)
    return jnp.where(n < exact, n, large)


def _bias_of_dist(rel_bias, dist):
    return jnp.moveaxis(rel_bias[_rel_bucket(dist)], -1, 0)


def _ssd_body(xs_ref, bm_ref, cm_ref, z_ref, dt_ref, cwx_ref, cwb_ref, cwc_ref, cbx_ref, cbb_ref, cbc_ref,
              dtb_ref, alog_ref, dskip_ref, ng_ref, o_ref, cat_ref, state_ref):
    c = pl.program_id(2)
    L = xs_ref.shape[0]
    HG = SSD_HEADS // SSD_GROUPS
    P = SSD_HEAD_DIM
    WX = HG * P

    @pl.when(c == 0)
    def _():
        cat_ref[0:8, :] = jnp.zeros((8, cat_ref.shape[1]), F32)
        state_ref[...] = jnp.zeros_like(state_ref)

    @pl.when(c > 0)
    def _():
        cat_ref[0:8, :] = cat_ref[L:L + 8, :]

    cat_ref[8:L + 8, 0:WX] = xs_ref[...]
    cat_ref[8:L + 8, WX:WX + SSD_STATE] = bm_ref[...]
    cat_ref[8:L + 8, WX + SSD_STATE:WX + 2 * SSD_STATE] = cm_ref[...]

    def conv(lo, hi, w_ref, b_ref):
        acc = b_ref[...] + w_ref[SSD_CONV - 1:SSD_CONV, :] * cat_ref[8:L + 8, lo:hi]
        for k in range(SSD_CONV - 1):
            s = SSD_CONV - 1 - k
            acc = acc + w_ref[k:k + 1, :] * cat_ref[pl.ds(8 - s, L), lo:hi]
        return acc * jax.nn.sigmoid(acc)

    xs = conv(0, WX, cwx_ref, cbx_ref)
    bm = conv(WX, WX + SSD_STATE, cwb_ref, cbb_ref)
    cm = conv(WX + SSD_STATE, WX + 2 * SSD_STATE, cwc_ref, cbc_ref)

    dt = jax.nn.softplus(dt_ref[...] + dtb_ref[...])
    a = -jnp.exp(alog_ref[...])
    row = lax.broadcasted_iota(jnp.int32, (L, L), 0)
    col = lax.broadcasted_iota(jnp.int32, (L, L), 1)
    causal = col <= row
    tril = causal.astype(F32)
    acum = _dot_hi(tril, dt * a)
    acum_t = acum.T
    er = lax.broadcasted_iota(jnp.int32, (LANES, WX), 0)
    ec = lax.broadcasted_iota(jnp.int32, (LANES, WX), 1)
    expand = (er == ec // P).astype(F32)
    dt_e = _dot_hi(dt, expand)
    acum_e = _dot_hi(acum, expand)
    last_e = acum_e[L - 1:L, :]
    xdt = xs * dt_e
    cm_b = cm.astype(BF16)
    bm_b = bm.astype(BF16)
    cb = _dot_t(cm_b, bm_b)
    state = state_ref[...]
    y_inter = _dot(cm_b, state.astype(BF16)) * jnp.exp(acum_e)
    xdt_b = xdt.astype(BF16)
    parts = []
    for h in range(HG):
        seg = acum[:, h:h + 1] - acum_t[h:h + 1, :]
        dec = jnp.exp(jnp.where(causal, seg, NEG_INF))
        gmat = (cb * dec).astype(BF16)
        parts.append(_dot(gmat, xdt_b[:, h * P:(h + 1) * P]))
    y = jnp.concatenate(parts, axis=1) + y_inter
    xw = (xdt * jnp.exp(last_e - acum_e)).astype(BF16)
    state_ref[...] = state * jnp.exp(last_e) + _dot(bm.T.astype(BF16), xw)
    y = y + dskip_ref[...] * xs
    zz = z_ref[...]
    y = y * (zz * jax.nn.sigmoid(zz))
    o_ref[...] = _rms(y, ng_ref[...]).astype(o_ref.dtype)


def ssd_mixer(p, cols, batch, conv_w, conv_b, dt_bias, a_log, d_skip, norm_g):
    m = p.shape[0]
    t = m // batch
    L = SSD_CHUNK
    nc = t // L
    G = SSD_GROUPS
    HG = SSD_HEADS // G
    WX = HG * SSD_HEAD_DIM
    N = SSD_STATE
    inner = SSD_HEADS * SSD_HEAD_DIM
    zb, xb, db = cols["z"] // WX, cols["xbc"] // WX, cols["dt"] // LANES
    bb = (cols["xbc"] + inner) // N
    cb_ = (cols["xbc"] + inner + G * N) // N

    def rows(b, g, c):
        return b * nc + c

    pad8 = lambda v: jnp.pad(v.reshape(G, 1, HG), ((0, 0), (0, 0), (0, LANES - HG)))
    cw = conv_w.astype(F32)
    cbias = conv_b.astype(F32).reshape(1, -1)
    cwx, cwb, cwc = cw[:, :inner], cw[:, inner:inner + G * N], cw[:, inner + G * N:]
    cbx, cbb, cbc = cbias[:, :inner], cbias[:, inner:inner + G * N], cbias[:, inner + G * N:]
    dsk = jnp.repeat(d_skip.astype(F32), SSD_HEAD_DIM).reshape(G, 1, WX)
    return pl.pallas_call(
        _ssd_body,
        grid=(batch, G, nc),
        in_specs=[
            pl.BlockSpec((L, WX), lambda b, g, c: (rows(b, g, c), xb + g)),
            pl.BlockSpec((L, N), lambda b, g, c: (rows(b, g, c), bb + g)),
            pl.BlockSpec((L, N), lambda b, g, c: (rows(b, g, c), cb_ + g)),
            pl.BlockSpec((L, WX), lambda b, g, c: (rows(b, g, c), zb + g)),
            pl.BlockSpec((L, LANES), lambda b, g, c: (rows(b, g, c), db + g)),
            pl.BlockSpec((SSD_CONV, WX), lambda b, g, c: (0, g)),
            pl.BlockSpec((SSD_CONV, N), lambda b, g, c: (0, g)),
            pl.BlockSpec((SSD_CONV, N), lambda b, g, c: (0, g)),
            pl.BlockSpec((1, WX), lambda b, g, c: (0, g)),
            pl.BlockSpec((1, N), lambda b, g, c: (0, g)),
            pl.BlockSpec((1, N), lambda b, g, c: (0, g)),
            pl.BlockSpec((None, 1, LANES), lambda b, g, c: (g, 0, 0)),
            pl.BlockSpec((None, 1, LANES), lambda b, g, c: (g, 0, 0)),
            pl.BlockSpec((None, 1, WX), lambda b, g, c: (g, 0, 0)),
            pl.BlockSpec((1, WX), lambda b, g, c: (0, g)),
        ],
        out_specs=pl.BlockSpec((L, WX), lambda b, g, c: (rows(b, g, c), g)),
        out_shape=jax.ShapeDtypeStruct((m, inner), BF16),
        scratch_shapes=[pltpu.VMEM((L + 8, WX + 2 * N), F32), pltpu.VMEM((N, WX), F32)],
        compiler_params=_cparams(("parallel", "parallel", "arbitrary")),
        name="ssd_mixer",
    )(p, p, p, p, p, cwx, cwb, cwc, cbx, cbb, cbc, pad8(dt_bias.astype(F32)), pad8(a_log.astype(F32)), dsk,
      norm_g.astype(F32).reshape(1, inner))


def _nsa_compress_body(x_ref, pe_ref, w1_ref, w2_ref, o_ref):
    x = x_ref[...]
    nr = x.shape[0]
    p0 = _dot((x + pe_ref[0:1, :]).astype(BF16), w1_ref[0])
    p1 = _dot((x + pe_ref[1:2, :]).astype(BF16), w1_ref[1])
    pre = p0 + pltpu.roll(p1, shift=nr - 1, axis=0)
    hdn = (pre * jax.nn.sigmoid(pre)).astype(BF16)
    out = _dot(hdn, w2_ref[...])
    valid = lax.broadcasted_iota(jnp.int32, out.shape, 0) < nr - 1
    o_ref[...] = jnp.where(valid, out, 0.0).astype(o_ref.dtype)


def nsa_compress(x, pe, w1, w2):
    b, _, g, nr, dk = x.shape
    hid = w1.shape[-1]
    dh = w2.shape[-1]
    return pl.pallas_call(
        _nsa_compress_body,
        grid=(b, 2, g),
        in_specs=[
            pl.BlockSpec((None, None, None, nr, dk), lambda i, s, j: (i, s, j, 0, 0)),
            pl.BlockSpec((None, 2, dk), lambda i, s, j: (s, 0, 0)),
            pl.BlockSpec((None, 2, dk, hid), lambda i, s, j: (s, 0, 0, 0)),
            pl.BlockSpec((None, hid, dh), lambda i, s, j: (s, 0, 0)),
        ],
        out_specs=pl.BlockSpec((None, None, None, nr, dh), lambda i, s, j: (i, s, j, 0, 0)),
        out_shape=jax.ShapeDtypeStruct((b, 2, g, nr, dh), BF16),
        compiler_params=_cparams(("parallel", "parallel", "parallel")),
        name="nsa_compress",
    )(x, pe, w1, w2)


NSA_CMP_PAD = 16
NSA_SEL_PAD = 128
NSA_FAR_TILE = 512
NSA_NEAR = 256


def _softmax_parts(parts):
    sm = [jnp.where(m, s, NEG_INF) for s, m in parts]
    mx = functools.reduce(jnp.maximum, [jnp.max(s, axis=-1, keepdims=True) for s in sm])
    es = [jnp.where(m, jnp.exp(s - mx), 0.0) for s, (_, m) in zip(sm, parts)]
    l = functools.reduce(jnp.add, [jnp.sum(e, axis=-1, keepdims=True) for e in es])
    inv = jnp.where(l > 0.0, 1.0 / l, 0.0)
    return [e * inv for e in es]


def _nsa_body(cfar_ref, q_ref, gate_ref, kc_ref, vc_ref, ks_ref, vs_ref, kw_ref, vw_ref,
              tc_ref, ts_ref, tw_ref, ov_ref, o_ref, st_ref, *, n_top):
    g = pl.program_id(1)
    qb = pl.program_id(2)
    Q = Q_BLOCK
    R = NSA_HEADS // NSA_KV_GROUPS
    Dh = NSA_HEAD_DIM
    t0 = qb * Q
    nck = kc_ref.shape[0] - 128

    def rep(x):
        return jnp.concatenate([x] * R, axis=0)

    q = q_ref[...] * (Dh ** -0.5)
    qs = jnp.concatenate([q[:, r * Dh:(r + 1) * Dh] for r in range(R)], axis=0).astype(BF16)
    cfar = jnp.concatenate([jnp.full((Q, 1), cfar_ref[g * R + r], F32) for r in range(R)], axis=0)
    i_q = lax.broadcasted_iota(jnp.int32, (Q, 1), 0)

    ncmp = nck - NSA_CMP_PAD
    kc_all = kc_ref[NSA_CMP_PAD:nck, :]
    vc_all = vc_ref[NSA_CMP_PAD:nck, :]
    n_far = lax.broadcasted_iota(jnp.int32, (1, ncmp), 1)
    mask_far = rep(jnp.broadcast_to(n_far < 8 * qb - NSA_CMP_PAD, (Q, ncmp)))
    s_far = _dot_t(qs, kc_all) + cfar
    near0 = pl.multiple_of(8 * qb, 8)
    kc_near = kc_ref[pl.ds(near0, 128), :]
    vc_near = vc_ref[pl.ds(near0, 128), :]
    m_near = lax.broadcasted_iota(jnp.int32, (1, 128), 1)
    dist_c = i_q - NSA_CMP_STRIDE * (m_near - NSA_CMP_PAD) - (NSA_CMP_LEN - 1)
    mask_near = rep((dist_c >= 0) & (m_near + 8 * qb - NSA_CMP_PAD >= 0) & (m_near < 2 * NSA_CMP_PAD))
    s_near = _dot_t(qs, kc_near) + tc_ref[...].reshape(R * Q, 128)
    p_far, p_near = _softmax_parts([(s_far, mask_far), (s_near, mask_near)])
    p_far = p_far.astype(BF16)
    p_near = p_near.astype(BF16)
    o_c = _dot(p_far, vc_all) + _dot(p_near, vc_near)
    imp4 = _dot(p_far, ov_ref[NSA_CMP_PAD:nck, :]) + _dot(p_near, ov_ref[pl.ds(near0, 128), :])
    imp = functools.reduce(jnp.add, [imp4[r * Q:(r + 1) * Q] for r in range(R)])

    j_blk = lax.broadcasted_iota(jnp.int32, (1, 128), 1)
    t_q = t0 + i_q
    cur = 2 * qb + (i_q >= NSA_SEL_LEN).astype(jnp.int32)
    forced = (j_blk == 0) | (j_blk == cur) | (j_blk == cur - 1)
    adm = j_blk * NSA_SEL_LEN <= t_q
    score = jnp.where(adm, imp + NSA_FORCE * forced.astype(F32), NEG_INF)
    st_ref[...] = score.T
    st = st_ref[...]
    j_row = lax.broadcasted_iota(jnp.int32, (128, Q), 0)

    def rank_body(jp, cnt):
        row = st_ref[pl.ds(jp, 1), :]
        ahead = (row > st) | ((row == st) & (jp < j_row))
        return cnt + ahead.astype(jnp.int32)

    cnt = lax.fori_loop(0, 128, rank_body, jnp.zeros((128, Q), jnp.int32), unroll=8)
    sel_b = (cnt < n_top).astype(F32).T.astype(BF16)

    jr = lax.broadcasted_iota(jnp.int32, (128, NSA_FAR_TILE), 0)
    jc = lax.broadcasted_iota(jnp.int32, (128, NSA_FAR_TILE), 1)
    blk_delta = jr - jc // NSA_SEL_LEN
    c_far = lax.broadcasted_iota(jnp.int32, (1, NSA_FAR_TILE), 1)

    def far_body(kt, carry):
        m_i, l_i, acc = carry
        k0 = pl.multiple_of(kt * NSA_FAR_TILE, NSA_FAR_TILE)
        k = ks_ref[pl.ds(NSA_SEL_PAD + k0, NSA_FAR_TILE), :]
        v = vs_ref[pl.ds(NSA_SEL_PAD + k0, NSA_FAR_TILE), :]
        expand = (blk_delta == kt * (NSA_FAR_TILE // NSA_SEL_LEN)).astype(BF16)
        member = _dot(sel_b, expand) > 0.5
        mask = rep(member & (k0 + c_far < t0 - NSA_SEL_PAD))
        s = jnp.where(mask, _dot_t(qs, k) + cfar, NEG_INF)
        m_new = jnp.maximum(m_i, jnp.max(s, axis=-1, keepdims=True))
        p = jnp.where(mask, jnp.exp(s - m_new), 0.0)
        alpha = jnp.exp(m_i - m_new)
        l_new = alpha * l_i + jnp.sum(p, axis=-1, keepdims=True)
        acc_new = alpha * acc + _dot(p.astype(BF16), v)
        return m_new, l_new, acc_new

    n_far_tiles = (qb + 2) // 4
    init = (jnp.full((R * Q, 1), NEG_INF, F32), jnp.zeros((R * Q, 1), F32), jnp.zeros((R * Q, Dh), F32))
    m_i, l_i, acc = lax.fori_loop(0, n_far_tiles, far_body, init)

    t0a = pl.multiple_of(t0, Q)
    k = ks_ref[pl.ds(t0a, NSA_NEAR), :]
    v = vs_ref[pl.ds(t0a, NSA_NEAR), :]
    c_near = lax.broadcasted_iota(jnp.int32, (1, NSA_NEAR), 1)
    pick = (blk_delta[:, :NSA_NEAR] == 2 * qb - NSA_SEL_PAD // NSA_SEL_LEN).astype(BF16)
    member = _dot(sel_b, pick) > 0.5
    mask = rep(member & (c_near <= i_q + NSA_SEL_PAD) & (c_near >= NSA_SEL_PAD - t0))
    s = jnp.where(mask, _dot_t(qs, k) + ts_ref[...].reshape(R * Q, NSA_NEAR), NEG_INF)
    m_new = jnp.maximum(m_i, jnp.max(s, axis=-1, keepdims=True))
    p = jnp.where(mask, jnp.exp(s - m_new), 0.0)
    alpha = jnp.exp(m_i - m_new)
    l_s = alpha * l_i + jnp.sum(p, axis=-1, keepdims=True)
    o_s = (alpha * acc + _dot(p.astype(BF16), v)) * jnp.where(l_s > 0.0, 1.0 / l_s, 0.0)

    wlen = NSA_WINDOW + Q
    kwin = kw_ref[pl.ds(t0a, wlen), :]
    vwin = vw_ref[pl.ds(t0a, wlen), :]
    c_w = lax.broadcasted_iota(jnp.int32, (1, wlen), 1)
    dist_w = i_q + NSA_WINDOW - c_w
    mask_w = rep((dist_w >= 0) & (dist_w < NSA_WINDOW) & (c_w >= NSA_WINDOW - t0))
    s_w = _dot_t(qs, kwin) + tw_ref[...].reshape(R * Q, wlen)
    (p_w,) = _softmax_parts([(s_w, mask_w)])
    o_w = _dot(p_w.astype(BF16), vwin)

    gates = jax.nn.sigmoid(gate_ref[...])
    outs = []
    for r in range(R):
        rs = slice(r * Q, (r + 1) * Q)
        outs.append(gates[:, 3 * r:3 * r + 1] * o_c[rs] + gates[:, 3 * r + 1:3 * r + 2] * o_s[rs]
                    + gates[:, 3 * r + 2:3 * r + 3] * o_w[rs])
    o_ref[...] = jnp.concatenate(outs, axis=1).astype(o_ref.dtype)


def nsa_attention(p, cols, batch, cmp_kv, sel_k, sel_v, win_k, win_v, rel_bias):
    m = p.shape[0]
    t = m // batch
    G = NSA_KV_GROUPS
    R = NSA_HEADS // G
    Dh = NSA_HEAD_DIM
    Q = Q_BLOCK
    nq = t // Q
    ncmp = t // NSA_CMP_STRIDE
    n_sel = t // NSA_SEL_LEN
    assert n_sel <= 128 and t % NSA_FAR_TILE == 0
    n_top = min(NSA_SEL_BLOCKS, n_sel)
    wlen = NSA_WINDOW + Q

    cmp_p = jnp.pad(cmp_kv, ((0, 0), (0, 0), (0, 0), (NSA_CMP_PAD, 128), (0, 0)))
    pad_t = lambda a, n: jnp.pad(a, ((0, 0), (0, 0), (n, 0), (0, 0)))
    ks_p, vs_p = pad_t(sel_k, NSA_SEL_PAD), pad_t(sel_v, NSA_SEL_PAD)
    kw_p, vw_p = pad_t(win_k, NSA_WINDOW), pad_t(win_v, NSA_WINDOW)

    i_q = jnp.arange(Q)[:, None]
    rb = rel_bias.astype(F32)
    tc = _bias_of_dist(rb, i_q - NSA_CMP_STRIDE * (jnp.arange(128)[None, :] - NSA_CMP_PAD) - (NSA_CMP_LEN - 1))
    ts = _bias_of_dist(rb, i_q + NSA_SEL_PAD - jnp.arange(NSA_NEAR)[None, :])
    tw = _bias_of_dist(rb, i_q + NSA_WINDOW - jnp.arange(wlen)[None, :])
    cfar = rb[REL_BUCKETS - 1]
    n_idx = np.arange(ncmp + NSA_CMP_PAD + 128)[:, None] - NSA_CMP_PAD
    j_idx = np.arange(128)[None, :]
    ov = ((n_idx >= 0) & (n_idx < ncmp - 1) & (n_idx * NSA_CMP_STRIDE < (j_idx + 1) * NSA_SEL_LEN)
          & (n_idx * NSA_CMP_STRIDE + NSA_CMP_LEN - 1 >= j_idx * NSA_SEL_LEN))
    ov = jnp.asarray(ov, BF16)

    qblk = cols["q"] // (R * Dh)
    gblk = cols["gates"] // LANES
    kv_spec = lambda rows_: pl.BlockSpec((None, None, rows_, Dh), lambda b, g, i: (b, g, 0, 0))
    return pl.pallas_call(
        functools.partial(_nsa_body, n_top=n_top),
        grid=(batch, G, nq),
        in_specs=[
            pl.BlockSpec(memory_space=pltpu.SMEM),
            pl.BlockSpec((Q, R * Dh), lambda b, g, i: (b * nq + i, qblk + g)),
            pl.BlockSpec((Q, LANES), lambda b, g, i: (b * nq + i, gblk + g)),
            pl.BlockSpec((None, None, None, ncmp + NSA_CMP_PAD + 128, Dh), lambda b, g, i: (b, 0, g, 0, 0)),
            pl.BlockSpec((None, None, None, ncmp + NSA_CMP_PAD + 128, Dh), lambda b, g, i: (b, 1, g, 0, 0)),
            kv_spec(t + NSA_SEL_PAD), kv_spec(t + NSA_SEL_PAD),
            kv_spec(t + NSA_WINDOW), kv_spec(t + NSA_WINDOW),
            pl.BlockSpec((R, Q, 128), lambda b, g, i: (g, 0, 0)),
            pl.BlockSpec((R, Q, NSA_NEAR), lambda b, g, i: (g, 0, 0)),
            pl.BlockSpec((R, Q, wlen), lambda b, g, i: (g, 0, 0)),
            pl.BlockSpec((ncmp + NSA_CMP_PAD + 128, 128), lambda b, g, i: (0, 0)),
        ],
        out_specs=pl.BlockSpec((Q, R * Dh), lambda b, g, i: (b * nq + i, g)),
        out_shape=jax.ShapeDtypeStruct((m, NSA_HEADS * Dh), BF16),
        scratch_shapes=[pltpu.VMEM((128, Q), F32)],
        compiler_params=_cparams(("parallel", "parallel", "arbitrary")),
        name="nsa_attention",
    )(cfar, p, p, cmp_p, cmp_p, ks_p, vs_p, kw_p, vw_p, tc, ts, tw, ov)


def _dsa_prep_body(cq_ref, ckv_ref, kidx_ref, widx_ref, gq_ref, gkv_ref, wuq_ref, wuk_ref, wqi_ref,
                   qa_ref, qi_ref, ckvn_ref, ka_ref, kb_ref, w_ref):
    H, Dh = DSA_HEADS, DSA_HEAD_DIM
    cq = _rms(cq_ref[...], gq_ref[...]).astype(BF16)
    ckvn_ref[...] = _rms(ckv_ref[...], gkv_ref[...]).astype(BF16)
    q = _dot(cq, wuq_ref[...]).astype(BF16)
    for h in range(H):
        qa = _dot(q[:, h * Dh:(h + 1) * Dh], wuk_ref[h]) * (Dh ** -0.5)
        qa_ref[:, h * DSA_KV_RANK:(h + 1) * DSA_KV_RANK] = qa.astype(BF16)
    qi_ref[...] = _dot(cq, wqi_ref[...]).astype(BF16)
    kidx = kidx_ref[...]
    lane = lax.broadcasted_iota(jnp.int32, kidx.shape, 1)
    ka_ref[...] = jnp.where(lane < DSA_IDX_DIM, kidx, 0.0).astype(BF16)
    kb_ref[...] = jnp.where(lane >= DSA_IDX_DIM, pltpu.roll(kidx, shift=DSA_IDX_DIM, axis=1), 0.0).astype(BF16)
    w_ref[...] = widx_ref[...] * (DSA_IDX_HEADS ** -0.5 * DSA_IDX_DIM ** -0.5)


def dsa_prep(p, cols, gq, gkv, wuq, wuk, wqi, tm=512):
    m = p.shape[0]
    tm = min(tm, m)
    H, Dh, C = DSA_HEADS, DSA_HEAD_DIM, DSA_KV_RANK
    full = lambda a: pl.BlockSpec(a.shape, lambda i: (0,) * a.ndim)
    gq2, gkv2 = gq.reshape(1, -1), gkv.reshape(1, -1)
    outs = [
        jax.ShapeDtypeStruct((m, H * C), BF16),
        jax.ShapeDtypeStruct((m, DSA_IDX_HEADS * DSA_IDX_DIM), BF16),
        jax.ShapeDtypeStruct((m, C), BF16),
        jax.ShapeDtypeStruct((m, LANES), BF16),
        jax.ShapeDtypeStruct((m, LANES), BF16),
        jax.ShapeDtypeStruct((m, LANES), F32),
    ]
    return pl.pallas_call(
        _dsa_prep_body,
        grid=(m // tm,),
        in_specs=[
            pl.BlockSpec((tm, DSA_Q_RANK), lambda i: (i, cols["cq"] // DSA_Q_RANK)),
            pl.BlockSpec((tm, C), lambda i: (i, cols["ckv"] // C)),
            pl.BlockSpec((tm, LANES), lambda i: (i, cols["kidx"] // LANES)),
            pl.BlockSpec((tm, LANES), lambda i: (i, cols["widx"] // LANES)),
            full(gq2), full(gkv2), full(wuq), full(wuk), full(wqi),
        ],
        out_specs=[pl.BlockSpec((tm, o.shape[1]), lambda i: (i, 0)) for o in outs],
        out_shape=outs,
        compiler_params=_cparams(("parallel",)),
        name="dsa_prep",
    )(p, p, p, p, gq2, gkv2, wuq, wuk, wqi)


DSA_TILE = 512
DSA_CHUNK = 128


def _key_to_f32(key):
    bits = key ^ ((key >> 31) & jnp.int32(0x7FFFFFFF))
    return lax.bitcast_convert_type(bits, F32)


def _dsa_body(cfar_ref, qi_ref, w_ref, qa_ref, ka_ref, kb_ref, ckv_ref, tb_ref, wuv_ref, o_ref,
              sc_ref, m_ref, l_ref, acc_ref, *, n_keep, idx_bits):
    qb = pl.program_id(1)
    Q = Q_BLOCK
    H = DSA_HEADS
    C = DSA_KV_RANK
    TK = DSA_TILE
    CPT = TK // DSA_CHUNK
    t0 = qb * Q
    i_q = lax.broadcasted_iota(jnp.int32, (Q, 1), 0)
    t_q = t0 + i_q
    n_tiles = qb // CPT + 1

    sc_ref[0] = jnp.full((Q, DSA_CHUNK), -jnp.inf, F32)
    qi = qi_ref[...]
    npair = DSA_IDX_HEADS // 2
    qstack = jnp.concatenate([qi[:, j * LANES:(j + 1) * LANES] for j in range(npair)], axis=0)
    w = w_ref[...]
    c_tile = lax.broadcasted_iota(jnp.int32, (1, TK), 1)

    def idx_body(kt, carry):
        k0 = pl.multiple_of(kt * TK, TK)
        se = _dot_t(qstack, ka_ref[pl.ds(DSA_CHUNK + k0, TK), :])
        so = _dot_t(qstack, kb_ref[pl.ds(DSA_CHUNK + k0, TK), :])
        tot = jnp.zeros((Q, TK), F32)
        for j in range(npair):
            rs = slice(j * Q, (j + 1) * Q)
            tot = tot + jnp.maximum(se[rs], 0.0) * w[:, 2 * j:2 * j + 1]
            tot = tot + jnp.maximum(so[rs], 0.0) * w[:, 2 * j + 1:2 * j + 2]
        tot = jnp.where(k0 + c_tile <= t_q, tot + 0.0, -jnp.inf)
        for u in range(CPT):
            sc_ref[1 + kt * CPT + u] = tot[:, u * DSA_CHUNK:(u + 1) * DSA_CHUNK]
        return carry

    lax.fori_loop(0, n_tiles, idx_body, 0)

    def tile_scores(kt):
        blk = sc_ref[pl.ds(1 + kt * CPT, CPT)]
        return jnp.concatenate([blk[u] for u in range(CPT)], axis=1)

    c_chunk = lax.broadcasted_iota(jnp.int32, (1, DSA_CHUNK), 1)

    def count(pred):
        def body(kt, acc):
            blk = sc_ref[pl.ds(1 + kt * CPT, CPT)]
            for u in range(CPT):
                acc = acc + pred(blk[u], kt * TK + u * DSA_CHUNK + c_chunk).astype(F32)
            return acc
        acc = lax.fori_loop(0, n_tiles, body, jnp.zeros((Q, DSA_CHUNK), F32))
        return jnp.sum(acc, axis=-1, keepdims=True)

    keep = jnp.float32(n_keep)

    def bis_body(it, key):
        cand = key + jnp.left_shift(jnp.int32(1), 31 - it)
        cf = _key_to_f32(cand)
        ok = count(lambda x, kt: x >= cf) >= keep
        return jnp.where(ok, cand, key)

    key = lax.fori_loop(0, 32, bis_body, jnp.full((Q, 1), jnp.iinfo(jnp.int32).min, jnp.int32))
    few = t_q < n_keep
    thr = jnp.where(few, -jnp.inf, _key_to_f32(key))
    n_gt = count(lambda x, kt: x > thr)
    n_ge = count(lambda x, kt: x >= thr)
    need = keep - n_gt
    tied = jnp.max(jnp.where(few, 0.0, n_ge - n_gt - need)) > 0.0

    def tie_cut(_):
        def body(it, qcut):
            cand = qcut + jnp.left_shift(jnp.int32(1), idx_bits - 1 - it)
            n = count(lambda x, pos: (x == thr) & (pos < cand))
            return jnp.where(n < need, cand, qcut)
        return lax.fori_loop(0, idx_bits, body, jnp.zeros((Q, 1), jnp.int32))

    big = jnp.full((Q, 1), jnp.iinfo(jnp.int32).max, jnp.int32)
    qcut = lax.cond(tied, tie_cut, lambda _: big, 0)
    qcut = jnp.where(few, big, qcut)

    def member_of(x, s_pos):
        return (x > thr) | ((x == thr) & (s_pos <= qcut))

    HG = 4
    NG = H // HG
    c_near = lax.broadcasted_iota(jnp.int32, (1, 2 * DSA_CHUNK), 1)
    rep = lambda x: jnp.concatenate([x] * HG, axis=0)
    qas = [jnp.concatenate([qa_ref[:, (hg * HG + r) * C:(hg * HG + r + 1) * C] for r in range(HG)], axis=0)
           for hg in range(NG)]
    cfars = [jnp.concatenate([jnp.full((Q, 1), cfar_ref[hg * HG + r], F32) for r in range(HG)], axis=0)
             for hg in range(NG)]
    m_ref[...] = jnp.full(m_ref.shape, NEG_INF, F32)
    l_ref[...] = jnp.zeros(l_ref.shape, F32)
    acc_ref[...] = jnp.zeros(acc_ref.shape, F32)

    def step(hg, s, mask, kv):
        rs = slice(hg * HG * Q, (hg + 1) * HG * Q)
        s = jnp.where(mask, s, NEG_INF)
        m_i = m_ref[rs, :]
        m_new = jnp.maximum(m_i, jnp.max(s, axis=-1, keepdims=True))
        p = jnp.where(mask, jnp.exp(s - m_new), 0.0)
        alpha = jnp.exp(m_i - m_new)
        l_ref[rs, :] = alpha * l_ref[rs, :] + jnp.sum(p, axis=-1, keepdims=True)
        acc_ref[rs, :] = alpha * acc_ref[rs, :] + _dot(p.astype(BF16), kv)
        m_ref[rs, :] = m_new

    def far_body(kt, carry):
        k0 = pl.multiple_of(kt * TK, TK)
        kv = ckv_ref[pl.ds(DSA_CHUNK + k0, TK), :]
        s_pos = k0 + c_tile
        mask = rep(member_of(tile_scores(kt), s_pos) & (s_pos < t0 - DSA_CHUNK))
        for hg in range(NG):
            step(hg, _dot_t(qas[hg], kv) + cfars[hg], mask, kv)
        return carry

    lax.fori_loop(0, (qb + 2) // CPT, far_body, 0)
    t0a = pl.multiple_of(t0, Q)
    kv = ckv_ref[pl.ds(t0a, 2 * DSA_CHUNK), :]
    blk = sc_ref[pl.ds(qb, 2)]
    x = jnp.concatenate([blk[0], blk[1]], axis=1)
    s_pos = t0 - DSA_CHUNK + c_near
    mask = rep(member_of(x, s_pos) & (s_pos <= t_q) & (s_pos >= 0))
    for hg in range(NG):
        step(hg, _dot_t(qas[hg], kv) + tb_ref[hg * HG:(hg + 1) * HG].reshape(HG * Q, 2 * DSA_CHUNK), mask, kv)
    l = l_ref[...]
    o = (acc_ref[...] * jnp.where(l > 0.0, 1.0 / l, 0.0)).astype(BF16)
    outs = [_dot(o[h * Q:(h + 1) * Q], wuv_ref[h]) for h in range(H)]
    o_ref[...] = jnp.concatenate(outs, axis=1).astype(o_ref.dtype)


def dsa_attention(qa, qi, ckvn, ka, kb, widx, batch, rel_bias, wuv):
    m = qa.shape[0]
    t = m // batch
    Q = Q_BLOCK
    nq = t // Q
    H, C, Dh = DSA_HEADS, DSA_KV_RANK, DSA_HEAD_DIM
    assert t % DSA_TILE == 0
    n_keep = min(DSA_TOPK, t // 4)
    idx_bits = int(t).bit_length()
    padk = lambda a: jnp.pad(a.reshape(batch, t, -1), ((0, 0), (DSA_CHUNK, 0), (0, 0)))
    ka_p, kb_p, ckv_p = padk(ka), padk(kb), padk(ckvn)
    rb = rel_bias.astype(F32)
    tb = _bias_of_dist(rb, jnp.arange(Q)[:, None] + DSA_CHUNK - jnp.arange(2 * DSA_CHUNK)[None, :])
    cfar = rb[REL_BUCKETS - 1]
    HG = 4
    return pl.pallas_call(
        functools.partial(_dsa_body, n_keep=n_keep, idx_bits=idx_bits),
        grid=(batch, nq),
        in_specs=[
            pl.BlockSpec(memory_space=pltpu.SMEM),
            pl.BlockSpec((Q, qi.shape[1]), lambda b, i: (b * nq + i, 0)),
            pl.BlockSpec((Q, LANES), lambda b, i: (b * nq + i, 0)),
            pl.BlockSpec((Q, H * C), lambda b, i: (b * nq + i, 0)),
            pl.BlockSpec((None, t + DSA_CHUNK, LANES), lambda b, i: (b, 0, 0)),
            pl.BlockSpec((None, t + DSA_CHUNK, LANES), lambda b, i: (b, 0, 0)),
            pl.BlockSpec((None, t + DSA_CHUNK, C), lambda b, i: (b, 0, 0)),
            pl.BlockSpec((H, Q, 2 * DSA_CHUNK), lambda b, i: (0, 0, 0)),
            pl.BlockSpec((H, C, Dh), lambda b, i: (0, 0, 0)),
        ],
        out_specs=pl.BlockSpec((Q, H * Dh), lambda b, i: (b * nq + i, 0)),
        out_shape=jax.ShapeDtypeStruct((m, H * Dh), BF16),
        scratch_shapes=[
            pltpu.VMEM((t // DSA_CHUNK + 1, Q, DSA_CHUNK), F32),
            pltpu.VMEM((H * Q, 1), F32),
            pltpu.VMEM((H * Q, 1), F32),
            pltpu.VMEM((H * Q, C), F32),
        ],
        compiler_params=_cparams(("parallel", "arbitrary")),
        name="dsa_attention",
    )(cfar, qi, widx, qa, ka_p, kb_p, ckv_p, tb, wuv)


def _head_sum(x, hd):
    r = lax.broadcasted_iota(jnp.int32, (LANES, LANES), 0) // hd
    c = lax.broadcasted_iota(jnp.int32, (LANES, LANES), 1) // hd
    seg = (r == c).astype(F32)
    n = x.shape[1] // LANES
    return jnp.concatenate([_dot_hi(x[:, j * LANES:(j + 1) * LANES], seg) for j in range(n)], axis=1)


def _rwkv_prep_body(x_ref, prev_ref, mu_ref, w0_ref, w2_ref, a0_ref, a2_ref, g2_ref, kk_ref, ka_ref,
                    r_ref, g_cum_ref, g_exc_ref, k_ref, v_ref, a_ref, b_ref, g_ref, *, rows_per_seq):
    i = pl.program_id(0)
    C = RWKV_HEADS * RWKV_HEAD_DIM
    x = x_ref[...]
    tm = x.shape[0]
    first = (i * tm) % rows_per_seq == 0
    prev_row = jnp.where(first, 0.0, prev_ref[7:8, :])
    rolled = pltpu.roll(x, shift=1, axis=0)
    prev = jnp.where(lax.broadcasted_iota(jnp.int32, x.shape, 0) == 0, prev_row, rolled)
    x = x + (prev - x) * mu_ref[...]
    r, k, v = x[:, 0:C], x[:, C:2 * C], x[:, 2 * C:3 * C]
    wl = x[:, 3 * C:3 * C + RWKV_W_LORA]
    al = x[:, 3 * C + RWKV_W_LORA:3 * C + LANES]
    gl = x[:, 3 * C + LANES:3 * C + LANES + RWKV_G_LORA]
    w = -jax.nn.softplus(-(w0_ref[...] + _dot(jnp.tanh(wl).astype(BF16), w2_ref[...]))) - 0.5
    eta = jax.nn.sigmoid(a0_ref[...] + _dot(al.astype(BF16), a2_ref[...]))
    g_ref[...] = _dot(jax.nn.sigmoid(gl).astype(BF16), g2_ref[...])
    kk = k * kk_ref[...]
    kk = kk / jnp.maximum(jnp.sqrt(_head_sum(kk * kk, RWKV_HEAD_DIM)), 1e-12)
    r_ref[...] = r
    lw = -jnp.exp(w)
    ti = lax.broadcasted_iota(jnp.int32, (tm, tm), 0)
    si = lax.broadcasted_iota(jnp.int32, (tm, tm), 1)
    in_chunk = ((si <= ti) & (ti // RWKV_CHUNK == si // RWKV_CHUNK)).astype(F32)
    gcum = _dot_hi(in_chunk, lw)
    g_cum_ref[...] = gcum
    g_exc_ref[...] = gcum - lw
    k_ref[...] = k * (1.0 + (eta - 1.0) * ka_ref[...])
    v_ref[...] = v
    a_ref[...] = -kk
    b_ref[...] = kk * eta


def rwkv_prep(p, col, seq_len, mu, w0, w2, a0, a2, g2, k_k, k_a, tm=256):
    m = p.shape[0]
    C = RWKV_HEADS * RWKV_HEAD_DIM
    width = 3 * C + LANES + RWKV_G_LORA
    tm = min(tm, m)
    assert col % LANES == 0 and seq_len % tm == 0 and tm % RWKV_CHUNK == 0
    x = lax.slice_in_dim(p, col, col + width, axis=1)
    row = lambda a: a.astype(F32).reshape(1, -1)
    full = lambda a: pl.BlockSpec(a.shape, lambda i: (0,) * a.ndim)
    args = [row(mu), row(w0), w2.astype(BF16), row(a0), a2.astype(BF16), g2.astype(BF16), row(k_k), row(k_a)]
    out = jax.ShapeDtypeStruct((m, C), F32)
    return pl.pallas_call(
        functools.partial(_rwkv_prep_body, rows_per_seq=seq_len),
        grid=(m // tm,),
        in_specs=[pl.BlockSpec((tm, width), lambda i: (i, 0)),
                  pl.BlockSpec((8, width), lambda i: (jnp.maximum(i * (tm // 8) - 1, 0), 0))]
                 + [full(a) for a in args],
        out_specs=[pl.BlockSpec((tm, C), lambda i: (i, 0))] * 8,
        out_shape=[out] * 8,
        compiler_params=_cparams(("parallel",)),
        name="rwkv_prep",
    )(x, x, *args)


RWKV_HB = 4


RWKV_GROUPS_PER_STEP = 2


def _rwkv_scan_body(r_ref, g_ref, ge_ref, k_ref, v_ref, a_ref, b_ref, gt_ref, kt_ref, bt_ref, y_ref, st_ref):
    c = pl.program_id(1)
    nb, L, _ = r_ref.shape
    N = RWKV_HEAD_DIM
    HB = RWKV_HB
    W = HB * N
    assert L == N

    @pl.when(c == 0)
    def _():
        st_ref[...] = jnp.zeros_like(st_ref)

    ri = lax.broadcasted_iota(jnp.int32, (W, W), 0)
    ci = lax.broadcasted_iota(jnp.int32, (W, W), 1)
    own = (ri // N) == (ci // N)
    strict = (ci % L) < (ri % L)
    lower = (ci % L) <= (ri % L)
    eye = (ri == ci).astype(F32)
    n_sq = max(int(L - 1).bit_length() - 1, 0)

    def stack(x):
        return jnp.where(own, jnp.concatenate([x] * HB, axis=0), 0.0).astype(BF16)

    def rows_of_heads(x):
        return jnp.concatenate([x[:, h * N:(h + 1) * N] for h in range(HB)], axis=0)

    for bi in range(nb):
        ys = []
        for gi in range(RWKV_GROUPS_PER_STEP):
            ls = slice(gi * W, (gi + 1) * W)
            g = g_ref[bi, :, ls]
            ieg = jnp.exp(-g)
            at = stack(a_ref[bi, :, ls] * jnp.exp(ge_ref[bi, :, ls]))
            rt = stack(r_ref[bi, :, ls] * jnp.exp(g))
            bt = stack(b_ref[bi, :, ls] * ieg)
            kt = stack(k_ref[bi, :, ls] * ieg)
            v_b = rows_of_heads(v_ref[bi, :, ls]).astype(BF16)
            gt = gt_ref[bi, ls, :]
            wend = jnp.exp(gt[:, L - 1:L] - gt)
            bh = jnp.where(own, jnp.concatenate([bt_ref[bi, ls, :] * wend] * HB, axis=1), 0.0).astype(BF16)
            kh = jnp.where(own, jnp.concatenate([kt_ref[bi, ls, :] * wend] * HB, axis=1), 0.0).astype(BF16)
            aa = _dot_t(jnp.concatenate([at, rt], axis=0), jnp.concatenate([bt, kt], axis=0))
            a_ab = jnp.where(strict, aa[0:W, 0:W], 0.0)
            a_ak = jnp.where(strict, aa[0:W, W:2 * W], 0.0).astype(BF16)
            m_rb = jnp.where(lower, aa[W:2 * W, 0:W], 0.0).astype(BF16)
            m_rk = jnp.where(lower, aa[W:2 * W, W:2 * W], 0.0).astype(BF16)
            tinv = eye + a_ab
            pw = a_ab
            for _ in range(n_sq):
                pw_b = pw.astype(BF16)
                pw = _dot(pw_b, pw_b)
                tinv = tinv + _dot(tinv.astype(BF16), pw.astype(BF16))
            st = st_ref[bi, gi]
            st_b = st.astype(BF16)
            x = _dot(at, st_b) + _dot(a_ak, v_b)
            u_b = _dot(tinv.astype(BF16), x.astype(BF16)).astype(BF16)
            y = _dot(rt, st_b) + _dot(m_rb, u_b) + _dot(m_rk, v_b)
            st_ref[bi, gi] = jnp.exp(gt[:, L - 1:L]) * st + _dot(bh, u_b) + _dot(kh, v_b)
            ys += [y[h * L:(h + 1) * L] for h in range(HB)]
        y_ref[bi] = jnp.concatenate(ys, axis=1)


def rwkv_scan(r, g, ge, k, v, a, b, batch):
    m, C = r.shape
    t = m // batch
    L = RWKV_CHUNK
    nc = t // L
    W = RWKV_GROUPS_PER_STEP * RWKV_HB * RWKV_HEAD_DIM
    tmaj = pl.BlockSpec((batch, L, W), lambda h, c: (0, c, h))
    cmaj = pl.BlockSpec((batch, None, W, L), lambda h, c: (0, c, h, 0))
    seq = lambda z: z.reshape(batch, t, C)
    chan = lambda z: z.reshape(batch, nc, L, C).transpose(0, 1, 3, 2)
    y = pl.pallas_call(
        _rwkv_scan_body,
        grid=(C // W, nc),
        in_specs=[tmaj] * 7 + [cmaj] * 3,
        out_specs=tmaj,
        out_shape=jax.ShapeDtypeStruct((batch, t, C), F32),
        scratch_shapes=[pltpu.VMEM((batch, RWKV_GROUPS_PER_STEP, RWKV_HB * RWKV_HEAD_DIM, RWKV_HEAD_DIM), F32)],
        compiler_params=_cparams(("parallel", "arbitrary")),
        name="rwkv_scan",
    )(seq(r), seq(g), seq(ge), seq(k), seq(v), seq(a), seq(b), chan(g), chan(k), chan(b))
    return y.reshape(m, C)


def _rwkv_post_body(y_ref, r_ref, k_ref, v_ref, g_ref, rk_ref, lng_ref, lnb_ref, o_ref):
    N = RWKV_HEAD_DIM
    y = y_ref[...]
    mean = _head_sum(y, N) * (1.0 / N)
    yc = y - mean
    var = _head_sum(yc * yc, N) * (1.0 / N)
    yn = yc * lax.rsqrt(var + RWKV_LN_EPS) * lng_ref[...] + lnb_ref[...]
    bonus = _head_sum(r_ref[...] * k_ref[...] * rk_ref[...], N) * v_ref[...]
    o_ref[...] = ((yn + bonus) * g_ref[...]).astype(o_ref.dtype)


def rwkv_post(y, r, k, v, g, r_k, ln_g, ln_b, tm=256):
    m, C = y.shape
    tm = min(tm, m)
    row = lambda a: a.astype(F32).reshape(1, C)
    blk = pl.BlockSpec((tm, C), lambda i: (i, 0))
    par = pl.BlockSpec((1, C), lambda i: (0, 0))
    return pl.pallas_call(
        _rwkv_post_body,
        grid=(m // tm,),
        in_specs=[blk] * 5 + [par] * 3,
        out_specs=blk,
        out_shape=jax.ShapeDtypeStruct((m, C), BF16),
        compiler_params=_cparams(("parallel",)),
        name="rwkv_post",
    )(y, r, k, v, g, row(r_k), row(ln_g), row(ln_b))


IN_PROJ_TN = 768


def _layout(segments, tn=IN_PROJ_TN):
    idx, valid, offs, pos = [], [], {}, 0
    for name, src, width in segments:
        offs[name] = pos
        src = np.asarray(src)
        idx.append(np.concatenate([src, np.zeros(width - len(src), np.int64)]))
        valid.append(np.concatenate([np.ones(len(src), bool), np.zeros(width - len(src), bool)]))
        pos += width
    total = -(-pos // tn) * tn
    idx.append(np.zeros(total - pos, np.int64))
    valid.append(np.zeros(total - pos, bool))
    return np.concatenate(idx), np.concatenate(valid), offs, total


def _even_layout():
    nq = NSA_HEADS * NSA_HEAD_DIM
    nkv = NSA_KV_GROUPS * NSA_HEAD_DIM
    inner = SSD_HEADS * SSD_HEAD_DIM
    conv_dim = inner + 2 * SSD_GROUPS * SSD_STATE
    o_gates = nq + 6 * nkv
    o_z = o_gates + 3 * NSA_HEADS
    o_xbc = o_z + inner
    o_dt = o_xbc + conv_dim
    ar = np.arange
    hpg = 3 * NSA_HEADS // NSA_KV_GROUPS
    dpg = SSD_HEADS // SSD_GROUPS
    segs = [("q", ar(0, nq), nq), ("kv", ar(nq, o_gates), 6 * nkv), ("z", ar(o_z, o_xbc), inner),
            ("xbc", ar(o_xbc, o_dt), conv_dim)]
    segs += [("gates" if g == 0 else f"gates{g}", o_gates + ar(g * hpg, (g + 1) * hpg), LANES)
             for g in range(NSA_KV_GROUPS)]
    segs += [("dt" if g == 0 else f"dt{g}", o_dt + ar(g * dpg, (g + 1) * dpg), LANES) for g in range(SSD_GROUPS)]
    return _layout(segs)


def _odd_layout():
    C = RWKV_HEADS * RWKV_HEAD_DIM
    o_kv = DSA_Q_RANK
    o_ki = o_kv + DSA_KV_RANK
    o_wi = o_ki + DSA_IDX_DIM
    o_rw = o_wi + DSA_IDX_HEADS
    n_rw = 3 * C + RWKV_W_LORA + RWKV_A_LORA + RWKV_G_LORA
    ar = np.arange
    segs = [("cq", ar(0, o_kv), DSA_Q_RANK), ("ckv", ar(o_kv, o_ki), DSA_KV_RANK), ("kidx", ar(o_ki, o_wi), LANES),
            ("widx", ar(o_wi, o_rw), LANES), ("rwkv", ar(o_rw, o_rw + n_rw), n_rw)]
    return _layout(segs)


def _permute_cols(w, idx, valid):
    return jnp.where(jnp.asarray(valid)[None, :], jnp.take(w, jnp.asarray(idx), axis=1), 0.0).astype(BF16)


def _even_mixer(h, g_norm, batch, w_in, w_out, rel_bias, pe_k, pe_v, w1_k, w2_k, w1_v, w2_v,
                conv_w, conv_b, dt_bias, a_log, d_skip, ssd_norm_g, g_out):
    m = h.shape[0]
    t = m // batch
    G, Dh = NSA_KV_GROUPS, NSA_HEAD_DIM
    idx, valid, cols, _ = _even_layout()
    p = norm_in_proj(h, g_norm, _permute_cols(w_in, idx, valid), tn=IN_PROJ_TN)
    kv = lax.slice_in_dim(p, cols["kv"], cols["kv"] + 6 * G * Dh, axis=1).reshape(batch, t, 6, G, Dh)
    rows16 = lambda u: (u.reshape(batch, t // NSA_CMP_STRIDE, NSA_CMP_STRIDE, G, Dh).transpose(0, 3, 1, 2, 4)
                        .reshape(batch, G, t // NSA_CMP_STRIDE, NSA_CMP_STRIDE * Dh))
    x_cmp = jnp.stack([rows16(kv[:, :, 0]), rows16(kv[:, :, 1])], axis=1)
    half = NSA_CMP_STRIDE * Dh
    pe = jnp.stack([pe_k.reshape(2, half), pe_v.reshape(2, half)]).astype(F32)
    w1 = jnp.stack([w1_k.reshape(2, half, -1), w1_v.reshape(2, half, -1)]).astype(BF16)
    w2 = jnp.stack([w2_k, w2_v]).astype(BF16)
    cmp_kv = nsa_compress(x_cmp, pe, w1, w2)
    tok = lambda i: kv[:, :, i].transpose(0, 2, 1, 3).astype(BF16)
    o_a = nsa_attention(p, cols, batch, cmp_kv, tok(2), tok(3), tok(4), tok(5), rel_bias)
    o_b = ssd_mixer(p, cols, batch, conv_w, conv_b, dt_bias, a_log, d_skip, ssd_norm_g)
    ka = o_a.shape[1]
    return out_proj_residual(o_a, o_b, w_out[:ka].astype(BF16), w_out[ka:].astype(BF16), g_out, h)


def _odd_mixer(h, g_norm, batch, w_in, w_out, rel_bias, q_norm_g, kv_norm_g, w_uq, w_qidx, w_uk, w_uv,
               mu, w0, w2, a0, a2, g2, k_k, k_a, r_k, ln_g, ln_b, g_out):
    m = h.shape[0]
    t = m // batch
    idx, valid, cols, _ = _odd_layout()
    p = norm_in_proj(h, g_norm, _permute_cols(w_in, idx, valid), tn=IN_PROJ_TN)
    H, Dh = DSA_HEADS, DSA_HEAD_DIM
    qa, qi, ckvn, ka, kb, widx = dsa_prep(
        p, cols, q_norm_g.astype(F32), kv_norm_g.astype(F32), w_uq.reshape(DSA_Q_RANK, H * Dh).astype(BF16),
        w_uk.transpose(1, 2, 0).astype(BF16), w_qidx.reshape(DSA_Q_RANK, -1).astype(BF16))
    o_c = dsa_attention(qa, qi, ckvn, ka, kb, widx, batch, rel_bias, w_uv.transpose(1, 0, 2).astype(BF16))
    r, gc, ge, k, v, a, b, g = rwkv_prep(p, cols["rwkv"], t, mu, w0, w2, a0, a2, g2, k_k, k_a)
    y = rwkv_scan(r, gc, ge, k, v, a, b, batch)
    o_d = rwkv_post(y, r, k, v, g, r_k, ln_g, ln_b)
    kc = o_c.shape[1]
    return out_proj_residual(o_c, o_d, w_out[:kc].astype(BF16), w_out[kc:].astype(BF16), g_out, h)


def kernel(x, norm_g, ffn_w_gate, ffn_w_up, ffn_w_down, rel_bias, even_w_in, even_w_out, nsa_pe_k, nsa_pe_v,
           nsa_cmp_w1_k, nsa_cmp_w2_k, nsa_cmp_w1_v, nsa_cmp_w2_v, ssd_conv_w, ssd_conv_b, ssd_dt_bias, ssd_a_log,
           ssd_d, ssd_norm_g, odd_w_in, odd_w_out, dsa_q_norm_g, dsa_kv_norm_g, dsa_w_uq, dsa_w_qidx, dsa_w_uk,
           dsa_w_uv, rwkv_mu, rwkv_w0, rwkv_w2, rwkv_a0, rwkv_a2, rwkv_g2, rwkv_k_k, rwkv_k_a, rwkv_r_k,
           rwkv_ln_g, rwkv_ln_b):
    batch, t, d = x.shape
    depth = norm_g.shape[0]
    h = x.reshape(batch * t, d).astype(F32)
    for layer in range(depth):
        g = norm_g[layer].astype(F32)
        i = layer // 2
        ffn = lambda hh, s, gi, go: ffn_half_step(hh, gi, ffn_w_gate[layer, s].astype(BF16),
                                                  ffn_w_up[layer, s].astype(BF16), ffn_w_down[layer, s].astype(BF16), go)
        h = ffn(h, 0, g[0], g[1])
        if layer % 2 == 0:
            h = _even_mixer(h, g[2], batch, even_w_in[i], even_w_out[i], rel_bias, nsa_pe_k[i], nsa_pe_v[i],
                            nsa_cmp_w1_k[i], nsa_cmp_w2_k[i], nsa_cmp_w1_v[i], nsa_cmp_w2_v[i], ssd_conv_w[i],
                            ssd_conv_b[i], ssd_dt_bias[i], ssd_a_log[i], ssd_d[i], ssd_norm_g[i], g[3])
        else:
            h = _odd_mixer(h, g[2], batch, odd_w_in[i], odd_w_out[i], rel_bias, dsa_q_norm_g[i], dsa_kv_norm_g[i],
                           dsa_w_uq[i], dsa_w_qidx[i], dsa_w_uk[i], dsa_w_uv[i], rwkv_mu[i], rwkv_w0[i], rwkv_w2[i],
                           rwkv_a0[i], rwkv_a2[i], rwkv_g2[i], rwkv_k_k[i], rwkv_k_a[i], rwkv_r_k[i],
                           rwkv_ln_g[i], rwkv_ln_b[i], g[3])
        h = ffn(h, 1, g[4], g[5])
    return h.reshape(batch, t, d).astype(x.dtype)
```
